```python
import math
import jax
import jax.numpy as jnp
from jax import lax
import numpy as np

D_MODEL = 1024
BATCH = 8
SEQ = 2048
DEPTH = 4

D_PLE = 256
RWKV_WIDTH = D_MODEL // 2
RWKV_HEAD = 64
RWKV_HEADS = RWKV_WIDTH // RWKV_HEAD
DECAY_LORA = 64
AAA_LORA = 64
GATE_LORA = 128
GN_EPS = RWKV_HEAD * 1e-5
DIFF_WIDTH = D_MODEL // 2
DIFF_HEADS = 4
DIFF_HEAD_DIM = DIFF_WIDTH // DIFF_HEADS // 2
SUBLN_EPS = 1e-5
Q_BLOCK = 128
D_FF = 4 * D_MODEL
NORM_EPS = 1e-6
RWKV_COLS = 3 * RWKV_WIDTH + DECAY_LORA + AAA_LORA + GATE_LORA
ATTN_COLS = 3 * DIFF_WIDTH
GATE_COLS = 2 * D_MODEL
IN_COLS = RWKV_COLS + ATTN_COLS + GATE_COLS

kernel_name = "hybrid_rwkv7_diffattn_gated_block"


def rmsnorm(x, g, eps=NORM_EPS):
    xf = x.astype(jnp.float32)
    y = xf * lax.rsqrt(jnp.mean(xf * xf, axis=-1, keepdims=True) + eps)
    return (y * g.astype(jnp.float32)).astype(x.dtype)


def wkv7_scan(r, w, k, v, a, b):
    bsz, _, nh, n = r.shape

    def step(state, inp):
        r_t, w_t, k_t, v_t, a_t, b_t = inp
        sa = jnp.einsum('bhvk,bhk->bhv', state, a_t)
        state = (state * w_t[:, :, None, :]
                 + sa[..., None] * b_t[:, :, None, :]
                 + v_t[..., None] * k_t[:, :, None, :])
        y_t = jnp.einsum('bhvk,bhk->bhv', state, r_t)
        return state, y_t

    xs = tuple(jnp.moveaxis(t, 1, 0) for t in (r, w, k, v, a, b))
    state0 = jnp.zeros((bsz, nh, n, n), jnp.float32)
    _, ys = lax.scan(step, state0, xs)
    return jnp.moveaxis(ys, 0, 1)


def rwkv7_time_mix(cols, mu, w0, w2, a0, a2, g2, k_k, k_a, r_k, lnx_w, lnx_b):
    out_dtype = cols.dtype
    c = cols.astype(jnp.float32)
    bsz, seq, _ = c.shape
    prev = jnp.pad(c[:, :-1], ((0, 0), (1, 0), (0, 0)))
    c = c + (prev - c) * mu.astype(jnp.float32)
    o1 = RWKV_WIDTH
    o2 = 2 * RWKV_WIDTH
    o3 = 3 * RWKV_WIDTH
    o4 = o3 + DECAY_LORA
    o5 = o4 + AAA_LORA
    r, k, v, wd, ad, gd = jnp.split(c, [o1, o2, o3, o4, o5], axis=-1)
    f32 = lambda t: t.astype(jnp.float32)
    w = -jax.nn.softplus(-(f32(w0) + jnp.tanh(wd) @ f32(w2))) - 0.5
    decay = jnp.exp(-jnp.exp(w))
    a = jax.nn.sigmoid(f32(a0) + ad @ f32(a2))
    g = jax.nn.sigmoid(gd) @ f32(g2)
    heads = lambda t: t.reshape(bsz, seq, RWKV_HEADS, RWKV_HEAD)
    kk = heads(k * f32(k_k))
    kk = kk / jnp.maximum(jnp.sqrt(jnp.sum(kk * kk, axis=-1, keepdims=True)), 1e-12)
    k = k * (1.0 + (a - 1.0) * f32(k_a))
    r_h, k_h, v_h, a_h, w_h = heads(r), heads(k), heads(v), heads(a), heads(decay)
    y = wkv7_scan(r_h, w_h, k_h, v_h, -kk, kk * a_h)
    mean = jnp.mean(y, axis=-1, keepdims=True)
    var = jnp.mean(jnp.square(y - mean), axis=-1, keepdims=True)
    y = (y - mean) * lax.rsqrt(var + GN_EPS)
    y = y.reshape(bsz, seq, RWKV_WIDTH) * f32(lnx_w) + f32(lnx_b)
    bonus = jnp.sum(r_h * k_h * f32(r_k), axis=-1, keepdims=True) * v_h
    y = (y + bonus.reshape(bsz, seq, RWKV_WIDTH)) * g
    return y.astype(out_dtype)


def diff_attention(cols, lam_q1, lam_k1, lam_q2, lam_k2, subln_g, lambda_init):
    out_dtype = cols.dtype
    bsz, seq, _ = cols.shape
    q, k, v = jnp.split(cols.astype(jnp.float32), 3, axis=-1)
    q = q.reshape(bsz, seq, DIFF_HEADS, 2, DIFF_HEAD_DIM) * (DIFF_HEAD_DIM ** -0.5)
    k = k.reshape(bsz, seq, DIFF_HEADS, 2, DIFF_HEAD_DIM)
    v = v.reshape(bsz, seq, DIFF_HEADS, 2 * DIFF_HEAD_DIM)
    f32 = lambda t: t.astype(jnp.float32)
    lam = (jnp.exp(jnp.sum(f32(lam_q1) * f32(lam_k1)))
           - jnp.exp(jnp.sum(f32(lam_q2) * f32(lam_k2))) + lambda_init)
    slopes = 2.0 ** (-8.0 * jnp.arange(1, DIFF_HEADS + 1, dtype=jnp.float32) / DIFF_HEADS)
    outs = []
    for blk in range(seq // Q_BLOCK):
        q0 = blk * Q_BLOCK
        kend = q0 + Q_BLOCK
        qb = q[:, q0:kend]
        kb = k[:, :kend]
        vb = v[:, :kend]
        s = jnp.einsum('bqhcd,bkhcd->bhcqk', qb, kb)
        dist = (q0 + jnp.arange(Q_BLOCK))[:, None] - jnp.arange(kend)[None, :]
        s = s - slopes[None, :, None, None, None] * dist.astype(jnp.float32)
        s = jnp.where(dist >= 0, s, -jnp.inf)
        pr = jax.nn.softmax(s, axis=-1)
        attn = pr[:, :, 0] - lam * pr[:, :, 1]
        outs.append(jnp.einsum('bhqk,bkhe->bqhe', attn, vb))
    o = jnp.concatenate(outs, axis=1)
    o = rmsnorm(o, subln_g, SUBLN_EPS) * (1.0 - lambda_init)
    return o.reshape(bsz, seq, DIFF_WIDTH).astype(out_dtype)


def setup_inputs(seed: int = 0) -> dict:
    key = jax.random.key(seed)
    ks = jax.random.split(key, 32)
    L = DEPTH
    nrm = lambda k, shape, scale: jax.random.normal(k, shape, jnp.float32) * scale
    gain = lambda k, shape: 1.0 + 0.05 * jax.random.normal(k, shape, jnp.float32)
    return {
        "x": nrm(ks[0], (BATCH, SEQ, D_MODEL), 1.0),
        "p": nrm(ks[1], (DEPTH, BATCH, SEQ, D_PLE), 1.0),
        "norm_mix_g": gain(ks[2], (L, D_MODEL)),
        "w_in": nrm(ks[3], (L, D_MODEL, IN_COLS), D_MODEL ** -0.5),
        "rwkv_mu": jax.random.uniform(ks[4], (L, RWKV_COLS), jnp.float32),
        "rwkv_w0": jax.random.uniform(ks[5], (L, RWKV_WIDTH), jnp.float32, minval=-4.0, maxval=1.0),
        "rwkv_w2": nrm(ks[6], (L, DECAY_LORA, RWKV_WIDTH), 0.1 * DECAY_LORA ** -0.5),
        "rwkv_a0": nrm(ks[7], (L, RWKV_WIDTH), 0.1),
        "rwkv_a2": nrm(ks[8], (L, AAA_LORA, RWKV_WIDTH), 0.1 * AAA_LORA ** -0.5),
        "rwkv_g2": nrm(ks[9], (L, GATE_LORA, RWKV_WIDTH), GATE_LORA ** -0.5),
        "rwkv_k_k": 0.85 + 0.05 * jax.random.normal(ks[10], (L, RWKV_WIDTH), jnp.float32),
        "rwkv_k_a": gain(ks[11], (L, RWKV_WIDTH)),
        "rwkv_r_k": nrm(ks[12], (L, RWKV_HEADS, RWKV_HEAD), 0.1),
        "rwkv_lnx_w": gain(ks[13], (L, RWKV_WIDTH)),
        "rwkv_lnx_b": nrm(ks[14], (L, RWKV_WIDTH), 0.01),
        "lam_q1": nrm(ks[15], (L, DIFF_HEAD_DIM), 0.1),
        "lam_k1": nrm(ks[16], (L, DIFF_HEAD_DIM), 0.1),
        "lam_q2": nrm(ks[17], (L, DIFF_HEAD_DIM), 0.1),
        "lam_k2": nrm(ks[18], (L, DIFF_HEAD_DIM), 0.1),
        "diff_subln_g": gain(ks[19], (L, 2 * DIFF_HEAD_DIM)),
        "w_proj_a": nrm(ks[20], (L, RWKV_WIDTH, D_MODEL), RWKV_WIDTH ** -0.5),
        "w_proj_b": nrm(ks[21], (L, DIFF_WIDTH, D_MODEL), DIFF_WIDTH ** -0.5),
        "w_out": nrm(ks[22], (L, D_MODEL, D_MODEL), D_MODEL ** -0.5),
        "norm_mlp_g": gain(ks[23], (L, D_MODEL)),
        "w_ff1": nrm(ks[24], (L, D_MODEL, D_FF), D_MODEL ** -0.5),
        "w_ff2": nrm(ks[25], (L, D_FF, D_MODEL), D_FF ** -0.5),
        "norm_ple_g": gain(ks[26], (L, D_MODEL)),
        "w_ple": nrm(ks[27], (L, D_PLE, D_MODEL), D_PLE ** -0.5),
        "w_ple_gate": nrm(ks[28], (L, D_MODEL, D_MODEL), D_MODEL ** -0.5),
        "final_norm_g": gain(ks[29], (D_MODEL,)),
    }


def reference(x, p, norm_mix_g, w_in, rwkv_mu, rwkv_w0, rwkv_w2, rwkv_a0, rwkv_a2,
              rwkv_g2, rwkv_k_k, rwkv_k_a, rwkv_r_k, rwkv_lnx_w, rwkv_lnx_b,
              lam_q1, lam_k1, lam_q2, lam_k2, diff_subln_g, w_proj_a, w_proj_b,
              w_out, norm_mlp_g, w_ff1, w_ff2, norm_ple_g, w_ple, w_ple_gate,
              final_norm_g):
    for i in range(DEPTH):
        h = rmsnorm(x, norm_mix_g[i])
        u = h @ w_in[i]
        u_rwkv, u_attn, u_gate = jnp.split(u, [RWKV_COLS, RWKV_COLS + ATTN_COLS], axis=-1)
        o_a = rwkv7_time_mix(u_rwkv, rwkv_mu[i], rwkv_w0[i], rwkv_w2[i], rwkv_a0[i],
                             rwkv_a2[i], rwkv_g2[i], rwkv_k_k[i], rwkv_k_a[i],
                             rwkv_r_k[i], rwkv_lnx_w[i], rwkv_lnx_b[i])
        lambda_init = 0.8 - 0.6 * math.exp(-0.3 * i)
        o_b = diff_attention(u_attn, lam_q1[i], lam_k1[i], lam_q2[i], lam_k2[i],
                             diff_subln_g[i], lambda_init)
        g_a, g_b = jnp.split(jax.nn.sigmoid(u_gate), 2, axis=-1)
        merged = g_a * (o_a @ w_proj_a[i]) + g_b * (o_b @ w_proj_b[i])
        x = x + merged @ w_out[i]
        h = rmsnorm(x, norm_mlp_g[i])
        x = x + jnp.square(jax.nn.relu(h @ w_ff1[i])) @ w_ff2[i]
        gate = jax.nn.sigmoid(rmsnorm(x, norm_ple_g[i]) @ w_ple_gate[i])
        x = x + (p[i] @ w_ple[i]) * gate
    return rmsnorm(x, final_norm_g)
```

```python
import functools
import math

import jax
import jax.numpy as jnp
from jax import lax
from jax.experimental import pallas as pl
from jax.experimental.pallas import tpu as pltpu

F32 = jnp.float32
BF16 = jnp.bfloat16

NORM_EPS = 1e-6
SUBLN_EPS = 1e-5
RWKV_HEAD = 64
GN_EPS = RWKV_HEAD * 1e-5
DIFF_HEADS = 4
Q_BLOCK = 128

VMEM_LIMIT_BYTES = 52 * 1024 * 1024
MASK_VALUE = -1e30

_NT = (((1,), (1,)), ((), ()))
_TN = (((0,), (0,)), ((), ()))


def _params(*sem):
    return pltpu.CompilerParams(dimension_semantics=sem, vmem_limit_bytes=VMEM_LIMIT_BYTES)


def _const_spec(shape):
    zeros = (0,) * len(shape)
    return pl.BlockSpec(shape, lambda *_: zeros, pipeline_mode=pl.Buffered(1))


def _rms(x, g, eps):
    return x * lax.rsqrt(jnp.mean(x * x, axis=-1, keepdims=True) + eps) * g


def _sigmoid(x):
    return 1.0 / (1.0 + jnp.exp(-x))


def _dot(a, b):
    return jnp.dot(a, b, preferred_element_type=F32)


def _in_proj_kernel(x_ref, g_ref, w_ref, ur_ref, ua_ref, ug_ref, *, n_r, n_a, d_attn, q_scale):
    h = _rms(x_ref[...], g_ref[...], NORM_EPS).astype(BF16)
    ur_ref[...] = _dot(h, w_ref[:, :n_r])
    ua_ref[:, :d_attn] = (_dot(h, w_ref[:, n_r:n_r + d_attn]) * q_scale).astype(BF16)
    ua_ref[:, d_attn:] = _dot(h, w_ref[:, n_r + d_attn:n_r + n_a]).astype(BF16)
    ug_ref[...] = _sigmoid(_dot(h, w_ref[:, n_r + n_a:]))


def _in_proj(x, g, w, n_r, n_a, tm):
    n, d = x.shape
    n_g = w.shape[1] - n_r - n_a
    d_attn = n_a // 3
    q_scale = float((d_attn // DIFF_HEADS // 2) ** -0.5)
    kern = functools.partial(_in_proj_kernel, n_r=n_r, n_a=n_a, d_attn=d_attn, q_scale=q_scale)
    return pl.pallas_call(
        kern,
        grid=(n // tm,),
        in_specs=[pl.BlockSpec((tm, d), lambda i: (i, 0)),
                  _const_spec((1, d)),
                  _const_spec(w.shape)],
        out_specs=[pl.BlockSpec((tm, n_r), lambda i: (i, 0)),
                   pl.BlockSpec((tm, n_a), lambda i: (i, 0)),
                   pl.BlockSpec((tm, n_g), lambda i: (i, 0))],
        out_shape=[jax.ShapeDtypeStruct((n, n_r), F32),
                   jax.ShapeDtypeStruct((n, n_a), BF16),
                   jax.ShapeDtypeStruct((n, n_g), F32)],
        compiler_params=_params("parallel"),
        name="in_proj",
    )(x, g, w)


RWKV_CHUNK = 64


def _split_dot(tri, x):
    hi = x.astype(BF16)
    r1 = x - hi.astype(F32)
    mid = r1.astype(BF16)
    lo = (r1 - mid.astype(F32)).astype(BF16)
    return _dot(tri, hi) + _dot(tri, mid) + _dot(tri, lo)


def _rwkv_kernel(u_ref, mu_ref, w0_ref, a0_ref, kk_ref, ka_ref, rk_ref, lnw_ref, lnb_ref,
                 wwa_ref, g2_ref, o_ref, carry_ref, state_ref, y_ref, *, width, n_heads, n_chunks):
    C = RWKV_CHUNK
    N = RWKV_HEAD

    @pl.when(pl.program_id(1) == 0)
    def _():
        carry_ref[...] = jnp.zeros_like(carry_ref)
        state_ref[...] = jnp.zeros_like(state_ref)

    row = lax.broadcasted_iota(jnp.int32, (C, 1), 0)
    ti = lax.broadcasted_iota(jnp.int32, (C, C), 0)
    si = lax.broadcasted_iota(jnp.int32, (C, C), 1)
    tri = (si <= ti).astype(BF16)
    eye = (si == ti).astype(F32)
    t2 = lax.broadcasted_iota(jnp.int32, (C, 2 * C), 0)
    s2 = lax.broadcasted_iota(jnp.int32, (C, 2 * C), 1) % C
    lane128 = lax.broadcasted_iota(jnp.int32, (C, 128), 1)
    zeros_cn = jnp.zeros((C, N), BF16)

    def chunk(j, _):
        r0 = pl.multiple_of(j * C, C)
        c = u_ref[pl.ds(r0, C), :]
        prev = jnp.where(row == 0, carry_ref[0:1, :], pltpu.roll(c, 1, axis=0))
        carry_ref[0:1, :] = u_ref[pl.ds(r0 + C - 1, 1), :]
        x = c + (prev - c) * mu_ref[...]
        r = x[:, :width]
        k = x[:, width:2 * width]
        v = x[:, 2 * width:3 * width]
        wa = x[:, 3 * width:3 * width + 128]
        gd = x[:, 3 * width + 128:]
        wa = jnp.where(lane128 < 64, jnp.tanh(wa), wa)
        lora = _dot(wa.astype(BF16), wwa_ref[...])
        z = w0_ref[...] + lora[:, :width]
        w_log = jnp.minimum(z, 0.0) - jnp.log(1.0 + jnp.exp(-jnp.abs(z))) - 0.5
        logw = -jnp.exp(w_log)
        a_sig = _sigmoid(a0_ref[...] + lora[:, width:])
        g = _dot(_sigmoid(gd).astype(BF16), g2_ref[...])
        kk = k * kk_ref[...]
        k2 = k * (1.0 + (a_sig - 1.0) * ka_ref[...])
        rk = r * k2 * rk_ref[...]

        l_inc = _split_dot(tri, logw)
        l_end = l_inc[C - 1:C, :]
        e_exc = jnp.exp(l_inc - logw)
        r_t = r * jnp.exp(l_inc)
        e_inv = jnp.exp(-l_inc)
        d_end = jnp.exp(l_end - l_inc)
        g_end = jnp.exp(l_end)
        k_t = k2 * e_inv
        k_h = k2 * d_end

        for h in range(n_heads):
            sl = slice(h * N, (h + 1) * N)
            kk_n = kk[:, sl]
            kk_n = kk_n / jnp.maximum(jnp.sqrt(jnp.sum(kk_n * kk_n, axis=-1, keepdims=True)), 1e-12)
            b_vec = kk_n * a_sig[:, sl]
            a_t = -kk_n * e_exc[:, sl]
            b_t = b_vec * e_inv[:, sl]
            b_h = b_vec * d_end[:, sl]
            v_h = v[:, sl]
            v_bf = v_h.astype(BF16)
            r_bf = r_t[:, sl].astype(BF16)
            a_bf = a_t.astype(BF16)

            ar = jnp.concatenate([a_bf, r_bf], axis=0)
            bk = jnp.concatenate([b_t.astype(BF16), k_t[:, sl].astype(BF16)], axis=0)
            gm = lax.dot_general(ar, bk, _NT, preferred_element_type=F32)
            ga = jnp.where(s2 < t2, gm[:C], 0.0)
            gy = jnp.where(s2 <= t2, gm[C:], 0.0)

            nm = ga[:, :C]
            inv = eye + nm
            pw = nm
            for _ in range(int(math.log2(C)) - 1):
                pw_bf = pw.astype(BF16)
                pw = _dot(pw_bf, pw_bf)
                inv = inv + _dot(inv.astype(BF16), pw.astype(BF16))
            inv_bf = inv.astype(BF16)

            p1 = _dot(ga.astype(BF16), jnp.concatenate([zeros_cn, v_bf], axis=0))
            u0 = _dot(inv_bf, p1.astype(BF16))
            w_m = _dot(inv_bf, a_bf)

            s0 = state_ref[h]
            s0_bf = s0.astype(BF16)
            u = lax.dot_general(w_m.astype(BF16), s0_bf, _NT, preferred_element_type=F32) + u0
            uv = jnp.concatenate([u.astype(BF16), v_bf], axis=0)
            y = lax.dot_general(r_bf, s0_bf, _NT, preferred_element_type=F32) + _dot(gy.astype(BF16), uv)
            bk_h = jnp.concatenate([b_h.astype(BF16), k_h[:, sl].astype(BF16)], axis=0)
            state_ref[h] = s0 * g_end[:, sl] + lax.dot_general(uv, bk_h, _TN, preferred_element_type=F32)

            mean = jnp.mean(y, axis=-1, keepdims=True)
            yc = y - mean
            var = jnp.mean(yc * yc, axis=-1, keepdims=True)
            bonus = jnp.sum(rk[:, sl], axis=-1, keepdims=True) * v_h
            y_ref[:, sl] = yc * lax.rsqrt(var + GN_EPS) * lnw_ref[:, sl] + lnb_ref[:, sl] + bonus

        o_ref[pl.ds(r0, C), :] = (y_ref[...] * g).astype(o_ref.dtype)
        return 0

    lax.fori_loop(0, n_chunks, chunk, 0)


def _rwkv(ur, mu, w0, a0, k_k, k_a, r_k, lnw, lnb, wwa, g2, t_blk):
    bsz, seq, cols = ur.shape
    width = w0.shape[1]
    n_heads = width // RWKV_HEAD
    assert cols == 3 * width + 256 and wwa.shape == (128, 2 * width) and g2.shape == (128, width)
    kern = functools.partial(_rwkv_kernel, width=width, n_heads=n_heads, n_chunks=t_blk // RWKV_CHUNK)
    row_w = _const_spec((1, width))
    return pl.pallas_call(
        kern,
        grid=(bsz, seq // t_blk),
        in_specs=[pl.BlockSpec((None, t_blk, cols), lambda b, i: (b, i, 0)),
                  _const_spec((1, cols)), row_w, row_w, row_w, row_w, row_w, row_w, row_w,
                  _const_spec(wwa.shape), _const_spec(g2.shape)],
        out_specs=pl.BlockSpec((None, t_blk, width), lambda b, i: (b, i, 0)),
        out_shape=jax.ShapeDtypeStruct((bsz, seq, width), BF16),
        scratch_shapes=[pltpu.VMEM((8, cols), F32),
                        pltpu.VMEM((n_heads, RWKV_HEAD, RWKV_HEAD), F32),
                        pltpu.VMEM((RWKV_CHUNK, width), F32)],
        compiler_params=_params("parallel", "arbitrary"),
        name="rwkv7",
    )(ur, mu, w0, a0, k_k, k_a, r_k, lnw, lnb, wwa, g2)


def _attn_kernel(slope_ref, q_ref, k_ref, v_ref, lq1_ref, lk1_ref, lq2_ref, lk2_ref, sg_ref, o_ref,
                 *, tq, lambda_init):
    i = pl.program_id(2)
    slope = slope_ref[pl.program_id(1)]
    q = q_ref[...]
    lane = lax.broadcasted_iota(jnp.int32, q.shape, 1)
    half = q.shape[1] // 2
    zero = jnp.zeros_like(q)
    qq = jnp.concatenate([jnp.where(lane < half, q, zero), jnp.where(lane >= half, q, zero)], axis=0)
    d0 = (lax.broadcasted_iota(jnp.int32, (tq, tq), 0) - lax.broadcasted_iota(jnp.int32, (tq, tq), 1))
    d0 = jnp.concatenate([d0, d0], axis=0)

    def block(j, carry, masked):
        m, l, acc = carry
        k0 = pl.multiple_of(j * tq, tq)
        kb = k_ref[pl.ds(k0, tq), :]
        vb = v_ref[pl.ds(k0, tq), :]
        s = lax.dot_general(qq, kb, _NT, preferred_element_type=F32)
        dist = d0 + (i - j) * tq
        s = s - slope * dist.astype(F32)
        if masked:
            s = jnp.where(dist >= 0, s, MASK_VALUE)
        m_new = jnp.maximum(m, jnp.max(s, axis=-1, keepdims=True))
        alpha = jnp.exp(m - m_new)
        p = jnp.exp(s - m_new)
        l = alpha * l + jnp.sum(p, axis=-1, keepdims=True)
        acc = alpha * acc + _dot(p.astype(BF16), vb)
        return m_new, l, acc

    init = (jnp.full((2 * tq, 1), MASK_VALUE, F32), jnp.zeros((2 * tq, 1), F32),
            jnp.zeros((2 * tq, q.shape[1]), F32))
    carry = lax.fori_loop(0, i, lambda j, c: block(j, c, False), init)
    m, l, acc = block(i, carry, True)
    o = acc / l
    lam = (jnp.exp(jnp.sum(lq1_ref[...] * lk1_ref[...], axis=-1, keepdims=True))
           - jnp.exp(jnp.sum(lq2_ref[...] * lk2_ref[...], axis=-1, keepdims=True)) + lambda_init)
    o = o[:tq] - lam * o[tq:]
    o = _rms(o, sg_ref[...], SUBLN_EPS) * (1.0 - lambda_init)
    o_ref[...] = o.astype(o_ref.dtype)


def _attn(ua, slopes, lq1, lk1, lq2, lk2, subln_g, lambda_init, tq):
    bsz, seq, n_a = ua.shape
    d_attn = n_a // 3
    hw = d_attn // DIFF_HEADS
    kern = functools.partial(_attn_kernel, tq=tq, lambda_init=lambda_init)
    lam_spec = _const_spec(lq1.shape)
    return pl.pallas_call(
        kern,
        grid=(bsz, DIFF_HEADS, seq // tq),
        in_specs=[pl.BlockSpec(memory_space=pltpu.SMEM),
                  pl.BlockSpec((None, tq, hw), lambda b, h, i: (b, i, h)),
                  pl.BlockSpec((None, seq, hw), lambda b, h, i: (b, 0, DIFF_HEADS + h)),
                  pl.BlockSpec((None, seq, hw), lambda b, h, i: (b, 0, 2 * DIFF_HEADS + h)),
                  lam_spec, lam_spec, lam_spec, lam_spec, _const_spec(subln_g.shape)],
        out_specs=pl.BlockSpec((None, tq, hw), lambda b, h, i: (b, i, h)),
        out_shape=jax.ShapeDtypeStruct((bsz, seq, d_attn), BF16),
        compiler_params=_params("parallel", "parallel", "arbitrary"),
        name="diff_attn",
    )(slopes, ua, ua, ua, lq1, lk1, lq2, lk2, subln_g)


def _merge_kernel(x_ref, oa_ref, ob_ref, ug_ref, pa_ref, pb_ref, wo_ref, o_ref):
    d = x_ref.shape[1]
    merged = (ug_ref[:, :d] * _dot(oa_ref[...], pa_ref[...])
              + ug_ref[:, d:] * _dot(ob_ref[...], pb_ref[...]))
    o_ref[...] = x_ref[...] + _dot(merged.astype(BF16), wo_ref[...])


def _merge(x, oa, ob, ug, pa, pb, wo, tm):
    n, d = x.shape
    blk = lambda a: pl.BlockSpec((tm, a.shape[1]), lambda i: (i, 0))
    return pl.pallas_call(
        _merge_kernel,
        grid=(n // tm,),
        in_specs=[blk(x), blk(oa), blk(ob), blk(ug),
                  _const_spec(pa.shape), _const_spec(pb.shape), _const_spec(wo.shape)],
        out_specs=blk(x),
        out_shape=jax.ShapeDtypeStruct((n, d), F32),
        compiler_params=_params("parallel"),
        name="merge",
    )(x, oa, ob, ug, pa, pb, wo)


def _mlp_kernel(x_ref, g_ref, w1_ref, w2_ref, o_ref, *, ff_blk):
    x = x_ref[...]
    h = _rms(x, g_ref[...], NORM_EPS).astype(BF16)
    acc = x
    for c in range(w1_ref.shape[1] // ff_blk):
        a = jnp.maximum(_dot(h, w1_ref[:, c * ff_blk:(c + 1) * ff_blk]), 0.0)
        acc = acc + _dot((a * a).astype(BF16), w2_ref[c * ff_blk:(c + 1) * ff_blk, :])
    o_ref[...] = acc


def _mlp(x, g, w1, w2, tm, ff_blk):
    n, d = x.shape
    return pl.pallas_call(
        functools.partial(_mlp_kernel, ff_blk=ff_blk),
        grid=(n // tm,),
        in_specs=[pl.BlockSpec((tm, d), lambda i: (i, 0)), _const_spec((1, d)),
                  _const_spec(w1.shape), _const_spec(w2.shape)],
        out_specs=pl.BlockSpec((tm, d), lambda i: (i, 0)),
        out_shape=jax.ShapeDtypeStruct((n, d), F32),
        compiler_params=_params("parallel"),
        name="mlp",
    )(x, g, w1, w2)


def _ple_kernel(x_ref, p_ref, g_ref, wp_ref, wg_ref, fg_ref, o_ref, *, final_norm):
    x = x_ref[...]
    gate = _sigmoid(_dot(_rms(x, g_ref[...], NORM_EPS).astype(BF16), wg_ref[...]))
    x = x + _dot(p_ref[...].astype(BF16), wp_ref[...]) * gate
    if final_norm:
        x = _rms(x, fg_ref[...], NORM_EPS)
    o_ref[...] = x


def _ple(x, p, g, wp, wg, fg, final_norm, tm):
    n, d = x.shape
    return pl.pallas_call(
        functools.partial(_ple_kernel, final_norm=final_norm),
        grid=(n // tm,),
        in_specs=[pl.BlockSpec((tm, d), lambda i: (i, 0)), pl.BlockSpec((tm, p.shape[1]), lambda i: (i, 0)),
                  _const_spec((1, d)), _const_spec(wp.shape), _const_spec(wg.shape), _const_spec((1, d))],
        out_specs=pl.BlockSpec((tm, d), lambda i: (i, 0)),
        out_shape=jax.ShapeDtypeStruct((n, d), F32),
        compiler_params=_params("parallel"),
        name="ple",
    )(x, p, g, wp, wg, fg)


def _pick(n, pref):
    while n % pref:
        pref //= 2
    return pref


def kernel(x, p, norm_mix_g, w_in, rwkv_mu, rwkv_w0, rwkv_w2, rwkv_a0, rwkv_a2, rwkv_g2, rwkv_k_k, rwkv_k_a, rwkv_r_k, rwkv_lnx_w, rwkv_lnx_b, lam_q1, lam_k1, lam_q2, lam_k2, diff_subln_g, w_proj_a, w_proj_b, w_out, norm_mlp_g, w_ff1, w_ff2, norm_ple_g, w_ple, w_ple_gate, final_norm_g):
    bsz, seq, d = x.shape
    depth = w_in.shape[0]
    n = bsz * seq
    width = rwkv_w0.shape[1]
    n_r = rwkv_mu.shape[1]
    n_a = 3 * w_proj_b.shape[1]
    d_lora = rwkv_w2.shape[1]
    assert d_lora == 64 and rwkv_a2.shape[1] == 64 and rwkv_g2.shape[1] == 128

    tm = _pick(n, 512)
    tm_in = _pick(n, 256)
    t_blk = _pick(seq, 256)
    tq = _pick(seq, 256)
    ff_blk = _pick(w_ff1.shape[2], 1024)

    row = lambda a: a.reshape(1, -1).astype(F32)
    slopes = 2.0 ** (-8.0 * jnp.arange(1, DIFF_HEADS + 1, dtype=F32) / DIFF_HEADS)
    xf = x.reshape(n, d)
    for i in range(depth):
        wwa = jnp.zeros((128, 2 * width), F32)
        wwa = wwa.at[:64, :width].set(rwkv_w2[i]).at[64:, width:].set(rwkv_a2[i]).astype(BF16)
        ur, ua, ug = _in_proj(xf, row(norm_mix_g[i]), w_in[i].astype(BF16), n_r, n_a, tm_in)
        oa = _rwkv(ur.reshape(bsz, seq, n_r), row(rwkv_mu[i]), row(rwkv_w0[i]), row(rwkv_a0[i]),
                   row(rwkv_k_k[i]), row(rwkv_k_a[i]), row(rwkv_r_k[i]), row(rwkv_lnx_w[i]),
                   row(rwkv_lnx_b[i]), wwa, rwkv_g2[i].astype(BF16), t_blk)
        lambda_init = 0.8 - 0.6 * math.exp(-0.3 * i)
        ob = _attn(ua.reshape(bsz, seq, n_a), slopes, row(lam_q1[i]), row(lam_k1[i]), row(lam_q2[i]),
                   row(lam_k2[i]), row(diff_subln_g[i]), lambda_init, tq)
        xf = _merge(xf, oa.reshape(n, width), ob.reshape(n, n_a // 3), ug,
                    w_proj_a[i].astype(BF16), w_proj_b[i].astype(BF16), w_out[i].astype(BF16), tm)
        xf = _mlp(xf, row(norm_mlp_g[i]), w_ff1[i].astype(BF16), w_ff2[i].astype(BF16), tm, ff_blk)
        xf = _ple(xf, p[i].reshape(n, -1), row(norm_ple_g[i]), w_ple[i].astype(BF16),
                  w_ple_gate[i].astype(BF16), row(final_norm_g), i == depth - 1, tm)
    return xf.reshape(bsz, seq, d)
```

```python
import functools
import math

import jax
import jax.numpy as jnp
from jax import lax
from jax.experimental import pallas as pl
from jax.experimental.pallas import tpu as pltpu

F32 = jnp.float32
BF16 = jnp.bfloat16

NORM_EPS = 1e-6
SUBLN_EPS = 1e-5
RWKV_HEAD = 64
GN_EPS = RWKV_HEAD * 1e-5
DIFF_HEADS = 4
Q_BLOCK = 128

VMEM_LIMIT_BYTES = 52 * 1024 * 1024
MASK_VALUE = -1e30

_NT = (((1,), (1,)), ((), ()))
_TN = (((0,), (0,)), ((), ()))


def _params(*sem):
    return pltpu.CompilerParams(dimension_semantics=sem, vmem_limit_bytes=VMEM_LIMIT_BYTES)


def _const_spec(shape):
    zeros = (0,) * len(shape)
    return pl.BlockSpec(shape, lambda *_: zeros, pipeline_mode=pl.Buffered(1))


def _rms(x, g, eps):
    return x * lax.rsqrt(jnp.mean(x * x, axis=-1, keepdims=True) + eps) * g


def _sigmoid(x):
    return 1.0 / (1.0 + jnp.exp(-x))


def _dot(a, b):
    return jnp.dot(a, b, preferred_element_type=F32)


def _in_proj_kernel(x_ref, g_ref, w_ref, ur_ref, ua_ref, ug_ref, *, n_r, n_a, d_attn, q_scale):
    h = _rms(x_ref[...], g_ref[...], NORM_EPS).astype(BF16)
    ur_ref[...] = _dot(h, w_ref[:, :n_r])
    ua_ref[:, :d_attn] = (_dot(h, w_ref[:, n_r:n_r + d_attn]) * q_scale).astype(BF16)
    ua_ref[:, d_attn:] = _dot(h, w_ref[:, n_r + d_attn:n_r + n_a]).astype(BF16)
    ug_ref[...] = _sigmoid(_dot(h, w_ref[:, n_r + n_a:]))


def _in_proj(x, g, w, n_r, n_a, tm):
    n, d = x.shape
    n_g = w.shape[1] - n_r - n_a
    d_attn = n_a // 3
    q_scale = float((d_attn // DIFF_HEADS // 2) ** -0.5)
    kern = functools.partial(_in_proj_kernel, n_r=n_r, n_a=n_a, d_attn=d_attn, q_scale=q_scale)
    return pl.pallas_call(
        kern,
        grid=(n // tm,),
        in_specs=[pl.BlockSpec((tm, d), lambda i: (i, 0)),
                  _const_spec((1, d)),
                  _const_spec(w.shape)],
        out_specs=[pl.BlockSpec((tm, n_r), lambda i: (i, 0)),
                   pl.BlockSpec((tm, n_a), lambda i: (i, 0)),
                   pl.BlockSpec((tm, n_g), lambda i: (i, 0))],
        out_shape=[jax.ShapeDtypeStruct((n, n_r), F32),
                   jax.ShapeDtypeStruct((n, n_a), BF16),
                   jax.ShapeDtypeStruct((n, n_g), F32)],
        compiler_params=_params("parallel"),
        name="in_proj",
    )(x, g, w)


RWKV_CHUNK = 64


def _split_dot(tri, x):
    hi = x.astype(BF16)
    r1 = x - hi.astype(F32)
    mid = r1.astype(BF16)
    lo = (r1 - mid.astype(F32)).astype(BF16)
    return _dot(tri, hi) + _dot(tri, mid) + _dot(tri, lo)


def _rwkv_kernel(u_ref, mu_ref, w0_ref, a0_ref, kk_ref, ka_ref, rk_ref, lnw_ref, lnb_ref,
                 wwa_ref, g2_ref, o_ref, carry_ref, state_ref, *, width, n_heads, n_chunks):
    C = RWKV_CHUNK
    N = RWKV_HEAD

    @pl.when(pl.program_id(1) == 0)
    def _():
        carry_ref[...] = jnp.zeros_like(carry_ref)
        state_ref[...] = jnp.zeros_like(state_ref)

    row = lax.broadcasted_iota(jnp.int32, (C, 1), 0)
    ti = lax.broadcasted_iota(jnp.int32, (C, C), 0)
    si = lax.broadcasted_iota(jnp.int32, (C, C), 1)
    tri = (si <= ti).astype(BF16)
    eye = (si == ti).astype(F32)
    t2 = lax.broadcasted_iota(jnp.int32, (C, 2 * C), 0)
    s2 = lax.broadcasted_iota(jnp.int32, (C, 2 * C), 1) % C
    lane128 = lax.broadcasted_iota(jnp.int32, (C, 128), 1)
    zeros_cn = jnp.zeros((C, N), BF16)

    def chunk(j, _):
        r0 = pl.multiple_of(j * C, C)
        c = u_ref[pl.ds(r0, C), :]
        prev = jnp.where(row == 0, carry_ref[0:1, :], pltpu.roll(c, 1, axis=0))
        carry_ref[0:1, :] = u_ref[pl.ds(r0 + C - 1, 1), :]
        x = c + (prev - c) * mu_ref[...]
        r = x[:, :width]
        k = x[:, width:2 * width]
        v = x[:, 2 * width:3 * width]
        wa = x[:, 3 * width:3 * width + 128]
        gd = x[:, 3 * width + 128:]
        wa = jnp.where(lane128 < 64, jnp.tanh(wa), wa)
        lora = _dot(wa.astype(BF16), wwa_ref[...])
        z = w0_ref[...] + lora[:, :width]
        w_log = jnp.minimum(z, 0.0) - jnp.log(1.0 + jnp.exp(-jnp.abs(z))) - 0.5
        logw = -jnp.exp(w_log)
        a_sig = _sigmoid(a0_ref[...] + lora[:, width:])
        g = _dot(_sigmoid(gd).astype(BF16), g2_ref[...])
        kk = k * kk_ref[...]
        k2 = k * (1.0 + (a_sig - 1.0) * ka_ref[...])
        rk = r * k2 * rk_ref[...]

        l_inc = _split_dot(tri, logw)
        l_end = l_inc[C - 1:C, :]
        e_exc = jnp.exp(l_inc - logw)
        r_t = r * jnp.exp(l_inc)
        e_inv = jnp.exp(-l_inc)
        d_end = jnp.exp(l_end - l_inc)
        g_end = jnp.exp(l_end)
        k_t = k2 * e_inv
        k_h = k2 * d_end

        heads = range(n_heads)
        sls = [slice(h * N, (h + 1) * N) for h in heads]
        a_bf, r_bf, v_f, v_bf, bk_end, ga, gy = [], [], [], [], [], [], []
        for sl in sls:
            kk_n = kk[:, sl]
            kk_n = kk_n / jnp.maximum(jnp.sqrt(jnp.sum(kk_n * kk_n, axis=-1, keepdims=True)), 1e-12)
            b_vec = kk_n * a_sig[:, sl]
            a_bf.append((-kk_n * e_exc[:, sl]).astype(BF16))
            r_bf.append(r_t[:, sl].astype(BF16))
            v_f.append(v[:, sl])
            v_bf.append(v[:, sl].astype(BF16))
            bk_end.append(jnp.concatenate([(b_vec * d_end[:, sl]).astype(BF16),
                                           k_h[:, sl].astype(BF16)], axis=0))
            ar = jnp.concatenate([a_bf[-1], r_bf[-1]], axis=0)
            bk = jnp.concatenate([(b_vec * e_inv[:, sl]).astype(BF16), k_t[:, sl].astype(BF16)], axis=0)
            gm = lax.dot_general(ar, bk, _NT, preferred_element_type=F32)
            ga.append(jnp.where(s2 < t2, gm[:C], 0.0))
            gy.append(jnp.where(s2 <= t2, gm[C:], 0.0).astype(BF16))

        pw = [ga[h][:, :C] for h in heads]
        inv = [eye + pw[h] for h in heads]
        p1 = [_dot(ga[h].astype(BF16), jnp.concatenate([zeros_cn, v_bf[h]], axis=0)) for h in heads]
        for _ in range(int(math.log2(C)) - 1):
            pw_bf = [pw[h].astype(BF16) for h in heads]
            pw = [_dot(pw_bf[h], pw_bf[h]) for h in heads]
            inv = [inv[h] + _dot(inv[h].astype(BF16), pw[h].astype(BF16)) for h in heads]
        inv_bf = [inv[h].astype(BF16) for h in heads]
        u0 = [_dot(inv_bf[h], p1[h].astype(BF16)) for h in heads]
        w_m = [_dot(inv_bf[h], a_bf[h]).astype(BF16) for h in heads]

        s0 = [state_ref[h] for h in heads]
        s0_bf = [s0[h].astype(BF16) for h in heads]
        u = [lax.dot_general(w_m[h], s0_bf[h], _NT, preferred_element_type=F32) + u0[h] for h in heads]
        uv = [jnp.concatenate([u[h].astype(BF16), v_bf[h]], axis=0) for h in heads]
        y = [lax.dot_general(r_bf[h], s0_bf[h], _NT, preferred_element_type=F32) + _dot(gy[h], uv[h])
             for h in heads]
        for h in heads:
            state_ref[h] = (s0[h] * g_end[:, sls[h]]
                            + lax.dot_general(uv[h], bk_end[h], _TN, preferred_element_type=F32))

        ys = []
        for h in heads:
            sl = sls[h]
            mean = jnp.mean(y[h], axis=-1, keepdims=True)
            yc = y[h] - mean
            var = jnp.mean(yc * yc, axis=-1, keepdims=True)
            bonus = jnp.sum(rk[:, sl], axis=-1, keepdims=True) * v_f[h]
            ys.append(yc * lax.rsqrt(var + GN_EPS) * lnw_ref[:, sl] + lnb_ref[:, sl] + bonus)
        o_ref[pl.ds(r0, C), :] = (jnp.concatenate(ys, axis=1) * g).astype(o_ref.dtype)
        return 0

    lax.fori_loop(0, n_chunks, chunk, 0)


def _rwkv(ur, mu, w0, a0, k_k, k_a, r_k, lnw, lnb, wwa, g2, t_blk):
    bsz, seq, cols = ur.shape
    width = w0.shape[1]
    n_heads = width // RWKV_HEAD
    assert cols == 3 * width + 256 and wwa.shape == (128, 2 * width) and g2.shape == (128, width)
    kern = functools.partial(_rwkv_kernel, width=width, n_heads=n_heads, n_chunks=t_blk // RWKV_CHUNK)
    row_w = _const_spec((1, width))
    return pl.pallas_call(
        kern,
        grid=(bsz, seq // t_blk),
        in_specs=[pl.BlockSpec((None, t_blk, cols), lambda b, i: (b, i, 0)),
                  _const_spec((1, cols)), row_w, row_w, row_w, row_w, row_w, row_w, row_w,
                  _const_spec(wwa.shape), _const_spec(g2.shape)],
        out_specs=pl.BlockSpec((None, t_blk, width), lambda b, i: (b, i, 0)),
        out_shape=jax.ShapeDtypeStruct((bsz, seq, width), BF16),
        scratch_shapes=[pltpu.VMEM((8, cols), F32),
                        pltpu.VMEM((n_heads, RWKV_HEAD, RWKV_HEAD), F32)],
        compiler_params=_params("parallel", "arbitrary"),
        name="rwkv7",
    )(ur, mu, w0, a0, k_k, k_a, r_k, lnw, lnb, wwa, g2)


def _attn_kernel(slope_ref, q_ref, k_ref, v_ref, lq1_ref, lk1_ref, lq2_ref, lk2_ref, sg_ref, o_ref,
                 *, tq, lambda_init):
    i = pl.program_id(2)
    slope = slope_ref[pl.program_id(1)]
    q = q_ref[...]
    lane = lax.broadcasted_iota(jnp.int32, q.shape, 1)
    half = q.shape[1] // 2
    zero = jnp.zeros_like(q)
    qq = jnp.concatenate([jnp.where(lane < half, q, zero), jnp.where(lane >= half, q, zero)], axis=0)
    d0 = (lax.broadcasted_iota(jnp.int32, (tq, tq), 0) - lax.broadcasted_iota(jnp.int32, (tq, tq), 1))
    d0 = jnp.concatenate([d0, d0], axis=0)

    def block(j, carry, masked):
        m, l, acc = carry
        k0 = pl.multiple_of(j * tq, tq)
        kb = k_ref[pl.ds(k0, tq), :]
        vb = v_ref[pl.ds(k0, tq), :]
        s = lax.dot_general(qq, kb, _NT, preferred_element_type=F32)
        dist = d0 + (i - j) * tq
        s = s - slope * dist.astype(F32)
        if masked:
            s = jnp.where(dist >= 0, s, MASK_VALUE)
        m_new = jnp.maximum(m, jnp.max(s, axis=-1, keepdims=True))
        alpha = jnp.exp(m - m_new)
        p = jnp.exp(s - m_new)
        l = alpha * l + jnp.sum(p, axis=-1, keepdims=True)
        acc = alpha * acc + _dot(p.astype(BF16), vb)
        return m_new, l, acc

    init = (jnp.full((2 * tq, 1), MASK_VALUE, F32), jnp.zeros((2 * tq, 1), F32),
            jnp.zeros((2 * tq, q.shape[1]), F32))
    carry = lax.fori_loop(0, i, lambda j, c: block(j, c, False), init)
    m, l, acc = block(i, carry, True)
    o = acc / l
    lam = (jnp.exp(jnp.sum(lq1_ref[...] * lk1_ref[...], axis=-1, keepdims=True))
           - jnp.exp(jnp.sum(lq2_ref[...] * lk2_ref[...], axis=-1, keepdims=True)) + lambda_init)
    o = o[:tq] - lam * o[tq:]
    o = _rms(o, sg_ref[...], SUBLN_EPS) * (1.0 - lambda_init)
    o_ref[...] = o.astype(o_ref.dtype)


def _attn(ua, slopes, lq1, lk1, lq2, lk2, subln_g, lambda_init, tq):
    bsz, seq, n_a = ua.shape
    d_attn = n_a // 3
    hw = d_attn // DIFF_HEADS
    kern = functools.partial(_attn_kernel, tq=tq, lambda_init=lambda_init)
    lam_spec = _const_spec(lq1.shape)
    return pl.pallas_call(
        kern,
        grid=(bsz, DIFF_HEADS, seq // tq),
        in_specs=[pl.BlockSpec(memory_space=pltpu.SMEM),
                  pl.BlockSpec((None, tq, hw), lambda b, h, i: (b, i, h)),
                  pl.BlockSpec((None, seq, hw), lambda b, h, i: (b, 0, DIFF_HEADS + h)),
                  pl.BlockSpec((None, seq, hw), lambda b, h, i: (b, 0, 2 * DIFF_HEADS + h)),
                  lam_spec, lam_spec, lam_spec, lam_spec, _const_spec(subln_g.shape)],
        out_specs=pl.BlockSpec((None, tq, hw), lambda b, h, i: (b, i, h)),
        out_shape=jax.ShapeDtypeStruct((bsz, seq, d_attn), BF16),
        compiler_params=_params("parallel", "parallel", "arbitrary"),
        name="diff_attn",
    )(slopes, ua, ua, ua, lq1, lk1, lq2, lk2, subln_g)


def _merge_kernel(x_ref, oa_ref, ob_ref, ug_ref, pa_ref, pb_ref, wo_ref, o_ref):
    d = x_ref.shape[1]
    merged = (ug_ref[:, :d] * _dot(oa_ref[...], pa_ref[...])
              + ug_ref[:, d:] * _dot(ob_ref[...], pb_ref[...]))
    o_ref[...] = x_ref[...] + _dot(merged.astype(BF16), wo_ref[...])


def _merge(x, oa, ob, ug, pa, pb, wo, tm):
    n, d = x.shape
    blk = lambda a: pl.BlockSpec((tm, a.shape[1]), lambda i: (i, 0))
    return pl.pallas_call(
        _merge_kernel,
        grid=(n // tm,),
        in_specs=[blk(x), blk(oa), blk(ob), blk(ug),
                  _const_spec(pa.shape), _const_spec(pb.shape), _const_spec(wo.shape)],
        out_specs=blk(x),
        out_shape=jax.ShapeDtypeStruct((n, d), F32),
        compiler_params=_params("parallel"),
        name="merge",
    )(x, oa, ob, ug, pa, pb, wo)


def _mlp_kernel(x_ref, g_ref, w1_ref, w2_ref, o_ref, *, ff_blk):
    x = x_ref[...]
    h = _rms(x, g_ref[...], NORM_EPS).astype(BF16)
    acc = x
    for c in range(w1_ref.shape[1] // ff_blk):
        a = jnp.maximum(_dot(h, w1_ref[:, c * ff_blk:(c + 1) * ff_blk]), 0.0)
        acc = acc + _dot((a * a).astype(BF16), w2_ref[c * ff_blk:(c + 1) * ff_blk, :])
    o_ref[...] = acc


def _mlp(x, g, w1, w2, tm, ff_blk):
    n, d = x.shape
    return pl.pallas_call(
        functools.partial(_mlp_kernel, ff_blk=ff_blk),
        grid=(n // tm,),
        in_specs=[pl.BlockSpec((tm, d), lambda i: (i, 0)), _const_spec((1, d)),
                  _const_spec(w1.shape), _const_spec(w2.shape)],
        out_specs=pl.BlockSpec((tm, d), lambda i: (i, 0)),
        out_shape=jax.ShapeDtypeStruct((n, d), F32),
        compiler_params=_params("parallel"),
        name="mlp",
    )(x, g, w1, w2)


def _ple_kernel(x_ref, p_ref, g_ref, wp_ref, wg_ref, fg_ref, o_ref, *, final_norm):
    x = x_ref[...]
    gate = _sigmoid(_dot(_rms(x, g_ref[...], NORM_EPS).astype(BF16), wg_ref[...]))
    x = x + _dot(p_ref[...].astype(BF16), wp_ref[...]) * gate
    if final_norm:
        x = _rms(x, fg_ref[...], NORM_EPS)
    o_ref[...] = x


def _ple(x, p, g, wp, wg, fg, final_norm, tm):
    n, d = x.shape
    return pl.pallas_call(
        functools.partial(_ple_kernel, final_norm=final_norm),
        grid=(n // tm,),
        in_specs=[pl.BlockSpec((tm, d), lambda i: (i, 0)), pl.BlockSpec((tm, p.shape[1]), lambda i: (i, 0)),
                  _const_spec((1, d)), _const_spec(wp.shape), _const_spec(wg.shape), _const_spec((1, d))],
        out_specs=pl.BlockSpec((tm, d), lambda i: (i, 0)),
        out_shape=jax.ShapeDtypeStruct((n, d), F32),
        compiler_params=_params("parallel"),
        name="ple",
    )(x, p, g, wp, wg, fg)


def _pick(n, pref):
    while n % pref:
        pref //= 2
    return pref


def kernel(x, p, norm_mix_g, w_in, rwkv_mu, rwkv_w0, rwkv_w2, rwkv_a0, rwkv_a2, rwkv_g2, rwkv_k_k, rwkv_k_a, rwkv_r_k, rwkv_lnx_w, rwkv_lnx_b, lam_q1, lam_k1, lam_q2, lam_k2, diff_subln_g, w_proj_a, w_proj_b, w_out, norm_mlp_g, w_ff1, w_ff2, norm_ple_g, w_ple, w_ple_gate, final_norm_g):
    bsz, seq, d = x.shape
    depth = w_in.shape[0]
    n = bsz * seq
    width = rwkv_w0.shape[1]
    n_r = rwkv_mu.shape[1]
    n_a = 3 * w_proj_b.shape[1]
    d_lora = rwkv_w2.shape[1]
    assert d_lora == 64 and rwkv_a2.shape[1] == 64 and rwkv_g2.shape[1] == 128

    tm = _pick(n, 512)
    tm_in = _pick(n, 256)
    t_blk = _pick(seq, 256)
    tq = _pick(seq, 256)
    ff_blk = _pick(w_ff1.shape[2], 1024)

    row = lambda a: a.reshape(1, -1).astype(F32)
    slopes = 2.0 ** (-8.0 * jnp.arange(1, DIFF_HEADS + 1, dtype=F32) / DIFF_HEADS)
    xf = x.reshape(n, d)
    for i in range(depth):
        wwa = jnp.zeros((128, 2 * width), F32)
        wwa = wwa.at[:64, :width].set(rwkv_w2[i]).at[64:, width:].set(rwkv_a2[i]).astype(BF16)
        ur, ua, ug = _in_proj(xf, row(norm_mix_g[i]), w_in[i].astype(BF16), n_r, n_a, tm_in)
        oa = _rwkv(ur.reshape(bsz, seq, n_r), row(rwkv_mu[i]), row(rwkv_w0[i]), row(rwkv_a0[i]),
                   row(rwkv_k_k[i]), row(rwkv_k_a[i]), row(rwkv_r_k[i]), row(rwkv_lnx_w[i]),
                   row(rwkv_lnx_b[i]), wwa, rwkv_g2[i].astype(BF16), t_blk)
        lambda_init = 0.8 - 0.6 * math.exp(-0.3 * i)
        ob = _attn(ua.reshape(bsz, seq, n_a), slopes, row(lam_q1[i]), row(lam_k1[i]), row(lam_q2[i]),
                   row(lam_k2[i]), row(diff_subln_g[i]), lambda_init, tq)
        xf = _merge(xf, oa.reshape(n, width), ob.reshape(n, n_a // 3), ug,
                    w_proj_a[i].astype(BF16), w_proj_b[i].astype(BF16), w_out[i].astype(BF16), tm)
        xf = _mlp(xf, row(norm_mlp_g[i]), w_ff1[i].astype(BF16), w_ff2[i].astype(BF16), tm, ff_blk)
        xf = _ple(xf, p[i].reshape(n, -1), row(norm_ple_g[i]), w_ple[i].astype(BF16),
                  w_ple_gate[i].astype(BF16), row(final_norm_g), i == depth - 1, tm)
    return xf.reshape(bsz, seq, d)
```

```python
import functools
import math

import jax
import jax.numpy as jnp
from jax import lax
from jax.experimental import pallas as pl
from jax.experimental.pallas import tpu as pltpu

F32 = jnp.float32
BF16 = jnp.bfloat16

NORM_EPS = 1e-6
SUBLN_EPS = 1e-5
RWKV_HEAD = 64
GN_EPS = RWKV_HEAD * 1e-5
DIFF_HEADS = 4
Q_BLOCK = 128

VMEM_LIMIT_BYTES = 52 * 1024 * 1024
MASK_VALUE = -1e30

_NT = (((1,), (1,)), ((), ()))
_TN = (((0,), (0,)), ((), ()))


def _params(*sem):
    return pltpu.CompilerParams(dimension_semantics=sem, vmem_limit_bytes=VMEM_LIMIT_BYTES)


def _const_spec(shape):
    zeros = (0,) * len(shape)
    return pl.BlockSpec(shape, lambda *_: zeros, pipeline_mode=pl.Buffered(1))


def _rms(x, g, eps):
    return x * lax.rsqrt(jnp.mean(x * x, axis=-1, keepdims=True) + eps) * g


def _sigmoid(x):
    return 1.0 / (1.0 + jnp.exp(-x))


def _dot(a, b):
    return jnp.dot(a, b, preferred_element_type=F32)


def _in_proj_kernel(x_ref, g_ref, w_ref, ur_ref, ua_ref, ug_ref, *, n_r, n_a, d_attn, q_scale):
    h = _rms(x_ref[...], g_ref[...], NORM_EPS).astype(BF16)
    ur_ref[...] = _dot(h, w_ref[:, :n_r])
    ua_ref[:, :d_attn] = (_dot(h, w_ref[:, n_r:n_r + d_attn]) * q_scale).astype(BF16)
    ua_ref[:, d_attn:] = _dot(h, w_ref[:, n_r + d_attn:n_r + n_a]).astype(BF16)
    ug_ref[...] = _sigmoid(_dot(h, w_ref[:, n_r + n_a:]))


def _in_proj(x, g, w, n_r, n_a, tm):
    n, d = x.shape
    n_g = w.shape[1] - n_r - n_a
    d_attn = n_a // 3
    q_scale = float((d_attn // DIFF_HEADS // 2) ** -0.5)
    kern = functools.partial(_in_proj_kernel, n_r=n_r, n_a=n_a, d_attn=d_attn, q_scale=q_scale)
    return pl.pallas_call(
        kern,
        grid=(n // tm,),
        in_specs=[pl.BlockSpec((tm, d), lambda i: (i, 0)),
                  _const_spec((1, d)),
                  _const_spec(w.shape)],
        out_specs=[pl.BlockSpec((tm, n_r), lambda i: (i, 0)),
                   pl.BlockSpec((tm, n_a), lambda i: (i, 0)),
                   pl.BlockSpec((tm, n_g), lambda i: (i, 0))],
        out_shape=[jax.ShapeDtypeStruct((n, n_r), F32),
                   jax.ShapeDtypeStruct((n, n_a), BF16),
                   jax.ShapeDtypeStruct((n, n_g), F32)],
        compiler_params=_params("parallel"),
        name="in_proj",
    )(x, g, w)


RWKV_CHUNK = 64


def _split_dot(tri, x):
    hi = x.astype(BF16)
    r1 = x - hi.astype(F32)
    mid = r1.astype(BF16)
    lo = (r1 - mid.astype(F32)).astype(BF16)
    return _dot(tri, hi) + _dot(tri, mid) + _dot(tri, lo)


def _rwkv_kernel(u_ref, mu_ref, w0_ref, a0_ref, kk_ref, ka_ref, rk_ref, lnw_ref, lnb_ref,
                 wwa_ref, g2_ref, o_ref, carry_ref, state_ref, *, width, n_heads, n_chunks):
    C = RWKV_CHUNK
    N = RWKV_HEAD

    @pl.when(pl.program_id(1) == 0)
    def _():
        carry_ref[...] = jnp.zeros_like(carry_ref)
        state_ref[...] = jnp.zeros_like(state_ref)

    row = lax.broadcasted_iota(jnp.int32, (C, 1), 0)
    ti = lax.broadcasted_iota(jnp.int32, (C, C), 0)
    si = lax.broadcasted_iota(jnp.int32, (C, C), 1)
    tri = (si <= ti).astype(BF16)
    eye = (si == ti).astype(F32)
    t2 = lax.broadcasted_iota(jnp.int32, (C, 2 * C), 0)
    s2 = lax.broadcasted_iota(jnp.int32, (C, 2 * C), 1) % C
    lane128 = lax.broadcasted_iota(jnp.int32, (C, 128), 1)
    zeros_cn = jnp.zeros((C, N), BF16)

    def chunk(j, _):
        r0 = pl.multiple_of(j * C, C)
        c = u_ref[pl.ds(r0, C), :]
        prev = jnp.where(row == 0, carry_ref[0:1, :], pltpu.roll(c, 1, axis=0))
        carry_ref[0:1, :] = u_ref[pl.ds(r0 + C - 1, 1), :]
        x = c + (prev - c) * mu_ref[...]
        r = x[:, :width]
        k = x[:, width:2 * width]
        v = x[:, 2 * width:3 * width]
        wa = x[:, 3 * width:3 * width + 128]
        gd = x[:, 3 * width + 128:]
        wa = jnp.where(lane128 < 64, jnp.tanh(wa), wa)
        lora = _dot(wa.astype(BF16), wwa_ref[...])
        z = w0_ref[...] + lora[:, :width]
        w_log = jnp.minimum(z, 0.0) - jnp.log(1.0 + jnp.exp(-jnp.abs(z))) - 0.5
        logw = -jnp.exp(w_log)
        a_sig = _sigmoid(a0_ref[...] + lora[:, width:])
        g = _dot(_sigmoid(gd).astype(BF16), g2_ref[...])
        kk = k * kk_ref[...]
        k2 = k * (1.0 + (a_sig - 1.0) * ka_ref[...])
        rk = r * k2 * rk_ref[...]

        l_inc = _split_dot(tri, logw)
        l_end = l_inc[C - 1:C, :]
        e_exc = jnp.exp(l_inc - logw)
        r_t = r * jnp.exp(l_inc)
        e_inv = jnp.exp(-l_inc)
        d_end = jnp.exp(l_end - l_inc)
        g_end = jnp.exp(l_end)
        k_t = k2 * e_inv
        k_h = k2 * d_end

        heads = range(n_heads)
        sls = [slice(h * N, (h + 1) * N) for h in heads]
        a_bf, r_bf, v_f, v_bf, bk_end, ga, gy = [], [], [], [], [], [], []
        for sl in sls:
            kk_n = kk[:, sl]
            kk_n = kk_n / jnp.maximum(jnp.sqrt(jnp.sum(kk_n * kk_n, axis=-1, keepdims=True)), 1e-12)
            b_vec = kk_n * a_sig[:, sl]
            a_bf.append((-kk_n * e_exc[:, sl]).astype(BF16))
            r_bf.append(r_t[:, sl].astype(BF16))
            v_f.append(v[:, sl])
            v_bf.append(v[:, sl].astype(BF16))
            bk_end.append(jnp.concatenate([(b_vec * d_end[:, sl]).astype(BF16),
                                           k_h[:, sl].astype(BF16)], axis=0))
            ar = jnp.concatenate([a_bf[-1], r_bf[-1]], axis=0)
            bk = jnp.concatenate([(b_vec * e_inv[:, sl]).astype(BF16), k_t[:, sl].astype(BF16)], axis=0)
            gm = lax.dot_general(ar, bk, _NT, preferred_element_type=F32)
            ga.append(jnp.where(s2 < t2, gm[:C], 0.0))
            gy.append(jnp.where(s2 <= t2, gm[C:], 0.0).astype(BF16))

        pw = [ga[h][:, :C] for h in heads]
        inv = [eye + pw[h] for h in heads]
        p1 = [_dot(ga[h].astype(BF16), jnp.concatenate([zeros_cn, v_bf[h]], axis=0)) for h in heads]
        for _ in range(int(math.log2(C)) - 1):
            pw_bf = [pw[h].astype(BF16) for h in heads]
            pw = [_dot(pw_bf[h], pw_bf[h]) for h in heads]
            inv = [inv[h] + _dot(inv[h].astype(BF16), pw[h].astype(BF16)) for h in heads]
        inv_bf = [inv[h].astype(BF16) for h in heads]
        u0 = [_dot(inv_bf[h], p1[h].astype(BF16)) for h in heads]
        w_m = [_dot(inv_bf[h], a_bf[h]).astype(BF16) for h in heads]

        s0 = [state_ref[h] for h in heads]
        s0_bf = [s0[h].astype(BF16) for h in heads]
        u = [lax.dot_general(w_m[h], s0_bf[h], _NT, preferred_element_type=F32) + u0[h] for h in heads]
        uv = [jnp.concatenate([u[h].astype(BF16), v_bf[h]], axis=0) for h in heads]
        y = [lax.dot_general(r_bf[h], s0_bf[h], _NT, preferred_element_type=F32) + _dot(gy[h], uv[h])
             for h in heads]
        for h in heads:
            state_ref[h] = (s0[h] * g_end[:, sls[h]]
                            + lax.dot_general(uv[h], bk_end[h], _TN, preferred_element_type=F32))

        ys = []
        for h in heads:
            sl = sls[h]
            mean = jnp.mean(y[h], axis=-1, keepdims=True)
            yc = y[h] - mean
            var = jnp.mean(yc * yc, axis=-1, keepdims=True)
            bonus = jnp.sum(rk[:, sl], axis=-1, keepdims=True) * v_f[h]
            ys.append(yc * lax.rsqrt(var + GN_EPS) * lnw_ref[:, sl] + lnb_ref[:, sl] + bonus)
        o_ref[pl.ds(r0, C), :] = (jnp.concatenate(ys, axis=1) * g).astype(o_ref.dtype)
        return 0

    lax.fori_loop(0, n_chunks, chunk, 0)


def _rwkv(ur, mu, w0, a0, k_k, k_a, r_k, lnw, lnb, wwa, g2, t_blk):
    bsz, seq, cols = ur.shape
    width = w0.shape[1]
    n_heads = width // RWKV_HEAD
    assert cols == 3 * width + 256 and wwa.shape == (128, 2 * width) and g2.shape == (128, width)
    kern = functools.partial(_rwkv_kernel, width=width, n_heads=n_heads, n_chunks=t_blk // RWKV_CHUNK)
    row_w = _const_spec((1, width))
    return pl.pallas_call(
        kern,
        grid=(bsz, seq // t_blk),
        in_specs=[pl.BlockSpec((None, t_blk, cols), lambda b, i: (b, i, 0)),
                  _const_spec((1, cols)), row_w, row_w, row_w, row_w, row_w, row_w, row_w,
                  _const_spec(wwa.shape), _const_spec(g2.shape)],
        out_specs=pl.BlockSpec((None, t_blk, width), lambda b, i: (b, i, 0)),
        out_shape=jax.ShapeDtypeStruct((bsz, seq, width), BF16),
        scratch_shapes=[pltpu.VMEM((8, cols), F32),
                        pltpu.VMEM((n_heads, RWKV_HEAD, RWKV_HEAD), F32)],
        compiler_params=_params("parallel", "arbitrary"),
        name="rwkv7",
    )(ur, mu, w0, a0, k_k, k_a, r_k, lnw, lnb, wwa, g2)


def _attn_kernel(slope_ref, q_ref, k_ref, v_ref, lq1_ref, lk1_ref, lq2_ref, lk2_ref, sg_ref, o_ref,
                 vt_ref, bias_ref, qqt_ref, st0_ref, st1_ref, pt0_ref, pt1_ref, acc_ref, m_ref, l_ref,
                 alpha_ref, *, tq, lambda_init):
    i = pl.program_id(2)
    slope = slope_ref[pl.program_id(1)]
    n_kv = v_ref.shape[0] // tq
    st_refs = (st0_ref, st1_ref)
    pt_refs = (pt0_ref, pt1_ref)

    @pl.when(i == 0)
    def _():
        for c in range(n_kv):
            vt_ref[:, c * tq:(c + 1) * tq] = v_ref[c * tq:(c + 1) * tq, :].astype(F32).T.astype(BF16)
        bias_ref[...] = slope * lax.broadcasted_iota(jnp.int32, bias_ref.shape, 0).astype(F32)

    q = q_ref[...].astype(F32)
    lane = lax.broadcasted_iota(jnp.int32, q.shape, 1)
    half = q.shape[1] // 2
    qq = jnp.concatenate([jnp.where(lane < half, q, 0.0), jnp.where(lane >= half, q, 0.0)], axis=0)
    qqt_ref[...] = qq.T.astype(BF16)

    def scores(j):
        k0 = pl.multiple_of(j * tq, tq)
        return _dot(k_ref[pl.ds(k0, tq), :], qqt_ref[...]) + bias_ref[...]

    def values_t(j):
        return vt_ref[:, pl.ds(pl.multiple_of(j * tq, tq), tq)]

    def step(j, cur, masked):
        nxt = 1 - cur
        if not masked:
            st_refs[nxt][...] = scores(j + 1)
        acc_ref[...] = alpha_ref[...] * acc_ref[...] + _dot(values_t(jnp.maximum(j - 1, 0)), pt_refs[nxt][...])
        st = st_refs[cur][...]
        if masked:
            kpos = lax.broadcasted_iota(jnp.int32, st.shape, 0)
            qpos = lax.broadcasted_iota(jnp.int32, st.shape, 1) % tq
            st = jnp.where(kpos <= qpos, st, MASK_VALUE)
        c_j = slope * (j * tq).astype(F32)
        m = m_ref[...]
        m_new = jnp.maximum(m, jnp.max(st, axis=0, keepdims=True) + c_j)
        alpha = jnp.exp(m - m_new)
        pt = jnp.exp(st - (m_new - c_j))
        l_ref[...] = alpha * l_ref[...] + jnp.sum(pt, axis=0, keepdims=True)
        m_ref[...] = m_new
        alpha_ref[...] = alpha
        pt_refs[cur][...] = pt.astype(BF16)

    st0_ref[...] = scores(0)
    pt1_ref[...] = jnp.zeros_like(pt1_ref)
    acc_ref[...] = jnp.zeros_like(acc_ref)
    m_ref[...] = jnp.full_like(m_ref, MASK_VALUE)
    l_ref[...] = jnp.zeros_like(l_ref)
    alpha_ref[...] = jnp.ones_like(alpha_ref)

    def pair(p, _):
        step(2 * p, 0, False)
        step(2 * p + 1, 1, False)
        return 0

    lax.fori_loop(0, i // 2, pair, 0)
    odd = i % 2 == 1

    @pl.when(odd)
    def _():
        step(i - 1, 0, False)

    def finish(cur):
        step(i, cur, True)
        acc = alpha_ref[...] * acc_ref[...] + _dot(values_t(i), pt_refs[cur][...])
        ot = acc / l_ref[...]
        lam = (jnp.exp(jnp.sum(lq1_ref[...] * lk1_ref[...], axis=-1, keepdims=True))
               - jnp.exp(jnp.sum(lq2_ref[...] * lk2_ref[...], axis=-1, keepdims=True)) + lambda_init)
        o = (ot[:, :tq] - lam * ot[:, tq:]).T
        o = _rms(o, sg_ref[...], SUBLN_EPS) * (1.0 - lambda_init)
        o_ref[...] = o.astype(o_ref.dtype)

    @pl.when(jnp.logical_not(odd))
    def _():
        finish(0)

    @pl.when(odd)
    def _():
        finish(1)


def _attn(ua, slopes, lq1, lk1, lq2, lk2, subln_g, lambda_init, tq):
    bsz, seq, n_a = ua.shape
    d_attn = n_a // 3
    hw = d_attn // DIFF_HEADS
    kern = functools.partial(_attn_kernel, tq=tq, lambda_init=lambda_init)
    lam_spec = _const_spec(lq1.shape)
    return pl.pallas_call(
        kern,
        grid=(bsz, DIFF_HEADS, seq // tq),
        in_specs=[pl.BlockSpec(memory_space=pltpu.SMEM),
                  pl.BlockSpec((None, tq, hw), lambda b, h, i: (b, i, h)),
                  pl.BlockSpec((None, seq, hw), lambda b, h, i: (b, 0, DIFF_HEADS + h)),
                  pl.BlockSpec((None, seq, hw), lambda b, h, i: (b, 0, 2 * DIFF_HEADS + h)),
                  lam_spec, lam_spec, lam_spec, lam_spec, _const_spec(subln_g.shape)],
        out_specs=pl.BlockSpec((None, tq, hw), lambda b, h, i: (b, i, h)),
        out_shape=jax.ShapeDtypeStruct((bsz, seq, d_attn), BF16),
        scratch_shapes=[pltpu.VMEM((hw, seq), BF16),
                        pltpu.VMEM((tq, 2 * tq), F32),
                        pltpu.VMEM((hw, 2 * tq), BF16),
                        pltpu.VMEM((tq, 2 * tq), F32), pltpu.VMEM((tq, 2 * tq), F32),
                        pltpu.VMEM((tq, 2 * tq), BF16), pltpu.VMEM((tq, 2 * tq), BF16),
                        pltpu.VMEM((hw, 2 * tq), F32),
                        pltpu.VMEM((1, 2 * tq), F32), pltpu.VMEM((1, 2 * tq), F32),
                        pltpu.VMEM((1, 2 * tq), F32)],
        compiler_params=_params("parallel", "parallel", "arbitrary"),
        name="diff_attn",
    )(slopes, ua, ua, ua, lq1, lk1, lq2, lk2, subln_g)


def _merge_kernel(x_ref, oa_ref, ob_ref, ug_ref, pa_ref, pb_ref, wo_ref, o_ref):
    d = x_ref.shape[1]
    merged = (ug_ref[:, :d] * _dot(oa_ref[...], pa_ref[...])
              + ug_ref[:, d:] * _dot(ob_ref[...], pb_ref[...]))
    o_ref[...] = x_ref[...] + _dot(merged.astype(BF16), wo_ref[...])


def _merge(x, oa, ob, ug, pa, pb, wo, tm):
    n, d = x.shape
    blk = lambda a: pl.BlockSpec((tm, a.shape[1]), lambda i: (i, 0))
    return pl.pallas_call(
        _merge_kernel,
        grid=(n // tm,),
        in_specs=[blk(x), blk(oa), blk(ob), blk(ug),
                  _const_spec(pa.shape), _const_spec(pb.shape), _const_spec(wo.shape)],
        out_specs=blk(x),
        out_shape=jax.ShapeDtypeStruct((n, d), F32),
        compiler_params=_params("parallel"),
        name="merge",
    )(x, oa, ob, ug, pa, pb, wo)


def _mlp_kernel(x_ref, g_ref, w1_ref, w2_ref, o_ref, *, ff_blk):
    x = x_ref[...]
    h = _rms(x, g_ref[...], NORM_EPS).astype(BF16)
    acc = x
    for c in range(w1_ref.shape[1] // ff_blk):
        a = jnp.maximum(_dot(h, w1_ref[:, c * ff_blk:(c + 1) * ff_blk]), 0.0)
        acc = acc + _dot((a * a).astype(BF16), w2_ref[c * ff_blk:(c + 1) * ff_blk, :])
    o_ref[...] = acc


def _mlp(x, g, w1, w2, tm, ff_blk):
    n, d = x.shape
    return pl.pallas_call(
        functools.partial(_mlp_kernel, ff_blk=ff_blk),
        grid=(n // tm,),
        in_specs=[pl.BlockSpec((tm, d), lambda i: (i, 0)), _const_spec((1, d)),
                  _const_spec(w1.shape), _const_spec(w2.shape)],
        out_specs=pl.BlockSpec((tm, d), lambda i: (i, 0)),
        out_shape=jax.ShapeDtypeStruct((n, d), F32),
        compiler_params=_params("parallel"),
        name="mlp",
    )(x, g, w1, w2)


def _ple_kernel(x_ref, p_ref, g_ref, wp_ref, wg_ref, fg_ref, o_ref, *, final_norm):
    x = x_ref[...]
    gate = _sigmoid(_dot(_rms(x, g_ref[...], NORM_EPS).astype(BF16), wg_ref[...]))
    x = x + _dot(p_ref[...].astype(BF16), wp_ref[...]) * gate
    if final_norm:
        x = _rms(x, fg_ref[...], NORM_EPS)
    o_ref[...] = x


def _ple(x, p, g, wp, wg, fg, final_norm, tm):
    n, d = x.shape
    return pl.pallas_call(
        functools.partial(_ple_kernel, final_norm=final_norm),
        grid=(n // tm,),
        in_specs=[pl.BlockSpec((tm, d), lambda i: (i, 0)), pl.BlockSpec((tm, p.shape[1]), lambda i: (i, 0)),
                  _const_spec((1, d)), _const_spec(wp.shape), _const_spec(wg.shape), _const_spec((1, d))],
        out_specs=pl.BlockSpec((tm, d), lambda i: (i, 0)),
        out_shape=jax.ShapeDtypeStruct((n, d), F32),
        compiler_params=_params("parallel"),
        name="ple",
    )(x, p, g, wp, wg, fg)


def _pick(n, pref):
    while n % pref:
        pref //= 2
    return pref


def kernel(x, p, norm_mix_g, w_in, rwkv_mu, rwkv_w0, rwkv_w2, rwkv_a0, rwkv_a2, rwkv_g2, rwkv_k_k, rwkv_k_a, rwkv_r_k, rwkv_lnx_w, rwkv_lnx_b, lam_q1, lam_k1, lam_q2, lam_k2, diff_subln_g, w_proj_a, w_proj_b, w_out, norm_mlp_g, w_ff1, w_ff2, norm_ple_g, w_ple, w_ple_gate, final_norm_g):
    bsz, seq, d = x.shape
    depth = w_in.shape[0]
    n = bsz * seq
    width = rwkv_w0.shape[1]
    n_r = rwkv_mu.shape[1]
    n_a = 3 * w_proj_b.shape[1]
    d_lora = rwkv_w2.shape[1]
    assert d_lora == 64 and rwkv_a2.shape[1] == 64 and rwkv_g2.shape[1] == 128

    tm = _pick(n, 512)
    tm_in = _pick(n, 256)
    t_blk = _pick(seq, 256)
    tq = _pick(seq, 256)
    ff_blk = _pick(w_ff1.shape[2], 1024)

    row = lambda a: a.reshape(1, -1).astype(F32)
    slopes = 2.0 ** (-8.0 * jnp.arange(1, DIFF_HEADS + 1, dtype=F32) / DIFF_HEADS)
    xf = x.reshape(n, d)
    for i in range(depth):
        wwa = jnp.zeros((128, 2 * width), F32)
        wwa = wwa.at[:64, :width].set(rwkv_w2[i]).at[64:, width:].set(rwkv_a2[i]).astype(BF16)
        ur, ua, ug = _in_proj(xf, row(norm_mix_g[i]), w_in[i].astype(BF16), n_r, n_a, tm_in)
        oa = _rwkv(ur.reshape(bsz, seq, n_r), row(rwkv_mu[i]), row(rwkv_w0[i]), row(rwkv_a0[i]),
                   row(rwkv_k_k[i]), row(rwkv_k_a[i]), row(rwkv_r_k[i]), row(rwkv_lnx_w[i]),
                   row(rwkv_lnx_b[i]), wwa, rwkv_g2[i].astype(BF16), t_blk)
        lambda_init = 0.8 - 0.6 * math.exp(-0.3 * i)
        ob = _attn(ua.reshape(bsz, seq, n_a), slopes, row(lam_q1[i]), row(lam_k1[i]), row(lam_q2[i]),
                   row(lam_k2[i]), row(diff_subln_g[i]), lambda_init, tq)
        xf = _merge(xf, oa.reshape(n, width), ob.reshape(n, n_a // 3), ug,
                    w_proj_a[i].astype(BF16), w_proj_b[i].astype(BF16), w_out[i].astype(BF16), tm)
        xf = _mlp(xf, row(norm_mlp_g[i]), w_ff1[i].astype(BF16), w_ff2[i].astype(BF16), tm, ff_blk)
        xf = _ple(xf, p[i].reshape(n, -1), row(norm_ple_g[i]), w_ple[i].astype(BF16),
                  w_ple_gate[i].astype(BF16), row(final_norm_g), i == depth - 1, tm)
    return xf.reshape(bsz, seq, d)
```

```python
import functools
import math

import jax
import jax.numpy as jnp
from jax import lax
from jax.experimental import pallas as pl
from jax.experimental.pallas import tpu as pltpu

F32 = jnp.float32
BF16 = jnp.bfloat16

NORM_EPS = 1e-6
SUBLN_EPS = 1e-5
RWKV_HEAD = 64
GN_EPS = RWKV_HEAD * 1e-5
DIFF_HEADS = 4
Q_BLOCK = 128

VMEM_LIMIT_BYTES = 52 * 1024 * 1024
MASK_VALUE = -1e30

_NT = (((1,), (1,)), ((), ()))
_TN = (((0,), (0,)), ((), ()))


def _params(*sem):
    return pltpu.CompilerParams(dimension_semantics=sem, vmem_limit_bytes=VMEM_LIMIT_BYTES)


def _const_spec(shape):
    zeros = (0,) * len(shape)
    return pl.BlockSpec(shape, lambda *_: zeros, pipeline_mode=pl.Buffered(1))


def _rms(x, g, eps):
    return x * lax.rsqrt(jnp.mean(x * x, axis=-1, keepdims=True) + eps) * g


def _sigmoid(x):
    return 1.0 / (1.0 + jnp.exp(-x))


def _dot(a, b):
    return jnp.dot(a, b, preferred_element_type=F32)


def _in_proj_kernel(x_ref, g_ref, w_ref, ur_ref, ua_ref, ug_ref, *, n_r, n_a, d_attn, q_scale):
    h = _rms(x_ref[...], g_ref[...], NORM_EPS).astype(BF16)
    ur_ref[...] = _dot(h, w_ref[:, :n_r])
    ua_ref[:, :d_attn] = (_dot(h, w_ref[:, n_r:n_r + d_attn]) * q_scale).astype(BF16)
    ua_ref[:, d_attn:] = _dot(h, w_ref[:, n_r + d_attn:n_r + n_a]).astype(BF16)
    ug_ref[...] = _sigmoid(_dot(h, w_ref[:, n_r + n_a:]))


def _in_proj(x, g, w, n_r, n_a, tm):
    n, d = x.shape
    n_g = w.shape[1] - n_r - n_a
    d_attn = n_a // 3
    q_scale = float((d_attn // DIFF_HEADS // 2) ** -0.5)
    kern = functools.partial(_in_proj_kernel, n_r=n_r, n_a=n_a, d_attn=d_attn, q_scale=q_scale)
    return pl.pallas_call(
        kern,
        grid=(n // tm,),
        in_specs=[pl.BlockSpec((tm, d), lambda i: (i, 0)),
                  _const_spec((1, d)),
                  _const_spec(w.shape)],
        out_specs=[pl.BlockSpec((tm, n_r), lambda i: (i, 0)),
                   pl.BlockSpec((tm, n_a), lambda i: (i, 0)),
                   pl.BlockSpec((tm, n_g), lambda i: (i, 0))],
        out_shape=[jax.ShapeDtypeStruct((n, n_r), F32),
                   jax.ShapeDtypeStruct((n, n_a), BF16),
                   jax.ShapeDtypeStruct((n, n_g), F32)],
        compiler_params=_params("parallel"),
        name="in_proj",
    )(x, g, w)


RWKV_CHUNK = 64


def _split_dot(tri, x):
    hi = x.astype(BF16)
    r1 = x - hi.astype(F32)
    mid = r1.astype(BF16)
    lo = (r1 - mid.astype(F32)).astype(BF16)
    return _dot(tri, hi) + _dot(tri, mid) + _dot(tri, lo)


def _rwkv_kernel(u_ref, mu_ref, w0_ref, a0_ref, kk_ref, ka_ref, rk_ref, lnw_ref, lnb_ref,
                 wwa_ref, g2_ref, o_ref, carry_ref, state_ref, *, width, n_heads, n_chunks):
    C = RWKV_CHUNK
    N = RWKV_HEAD

    @pl.when(pl.program_id(1) == 0)
    def _():
        carry_ref[...] = jnp.zeros_like(carry_ref)
        state_ref[...] = jnp.zeros_like(state_ref)

    row = lax.broadcasted_iota(jnp.int32, (C, 1), 0)
    ti = lax.broadcasted_iota(jnp.int32, (C, C), 0)
    si = lax.broadcasted_iota(jnp.int32, (C, C), 1)
    tri = (si <= ti).astype(BF16)
    eye = (si == ti).astype(F32)
    t2 = lax.broadcasted_iota(jnp.int32, (C, 2 * C), 0)
    s2 = lax.broadcasted_iota(jnp.int32, (C, 2 * C), 1) % C
    lane128 = lax.broadcasted_iota(jnp.int32, (C, 128), 1)
    zeros_cn = jnp.zeros((C, N), BF16)

    heads = range(n_heads)
    chunks = range(n_chunks)
    sls = [slice(h * N, (h + 1) * N) for h in heads]


    pro = []
    for c in chunks:
        x = u_ref[c * C:(c + 1) * C, :]
        first = carry_ref[0:1, :] if c == 0 else u_ref[c * C - 1:c * C, :]
        prev = jnp.where(row == 0, first, pltpu.roll(x, 1, axis=0))
        x = x + (prev - x) * mu_ref[...]
        r = x[:, :width]
        k = x[:, width:2 * width]
        v = x[:, 2 * width:3 * width]
        wa = x[:, 3 * width:3 * width + 128]
        gd = x[:, 3 * width + 128:]
        wa = jnp.where(lane128 < 64, jnp.tanh(wa), wa)
        lora = _dot(wa.astype(BF16), wwa_ref[...])
        z = w0_ref[...] + lora[:, :width]
        w_log = jnp.minimum(z, 0.0) - jnp.log(1.0 + jnp.exp(-jnp.abs(z))) - 0.5
        logw = -jnp.exp(w_log)
        a_sig = _sigmoid(a0_ref[...] + lora[:, width:])
        k2 = k * (1.0 + (a_sig - 1.0) * ka_ref[...])
        l_inc = _split_dot(tri, logw)
        l_end = l_inc[C - 1:C, :]
        e_inv = jnp.exp(-l_inc)
        d_end = jnp.exp(l_end - l_inc)
        pro.append(dict(
            v=v, a_sig=a_sig, e_inv=e_inv, d_end=d_end,
            g=_dot(_sigmoid(gd).astype(BF16), g2_ref[...]),
            kk=k * kk_ref[...], rk=r * k2 * rk_ref[...],
            e_exc=jnp.exp(l_inc - logw), r_t=r * jnp.exp(l_inc),
            g_end=jnp.exp(l_end),
            k_t=k2 * e_inv, k_h=k2 * d_end))
    carry_ref[0:1, :] = u_ref[n_chunks * C - 1:n_chunks * C, :]

    a_bf, r_bf, v_bf, bk_end, ga, gy = ({} for _ in range(6))
    probs = [(c, h) for c in chunks for h in heads]
    for c, h in probs:
        p, sl = pro[c], sls[h]
        kk_n = p["kk"][:, sl]
        kk_n = kk_n / jnp.maximum(jnp.sqrt(jnp.sum(kk_n * kk_n, axis=-1, keepdims=True)), 1e-12)
        b_vec = kk_n * p["a_sig"][:, sl]
        a_bf[c, h] = (-kk_n * p["e_exc"][:, sl]).astype(BF16)
        r_bf[c, h] = p["r_t"][:, sl].astype(BF16)
        v_bf[c, h] = p["v"][:, sl].astype(BF16)
        bk_end[c, h] = jnp.concatenate([(b_vec * p["d_end"][:, sl]).astype(BF16),
                                        p["k_h"][:, sl].astype(BF16)], axis=0)
        ar = jnp.concatenate([a_bf[c, h], r_bf[c, h]], axis=0)
        bk = jnp.concatenate([(b_vec * p["e_inv"][:, sl]).astype(BF16), p["k_t"][:, sl].astype(BF16)], axis=0)
        gm = lax.dot_general(ar, bk, _NT, preferred_element_type=F32)
        ga[c, h] = jnp.where(s2 < t2, gm[:C], 0.0)
        gy[c, h] = jnp.where(s2 <= t2, gm[C:], 0.0).astype(BF16)

    pw = {q: ga[q][:, :C] for q in probs}
    inv = {q: eye + pw[q] for q in probs}
    p1 = {q: _dot(ga[q].astype(BF16), jnp.concatenate([zeros_cn, v_bf[q]], axis=0)) for q in probs}
    for _ in range(int(math.log2(C)) - 1):
        pw_bf = {q: pw[q].astype(BF16) for q in probs}
        pw = {q: _dot(pw_bf[q], pw_bf[q]) for q in probs}
        inv = {q: inv[q] + _dot(inv[q].astype(BF16), pw[q].astype(BF16)) for q in probs}
    inv_bf = {q: inv[q].astype(BF16) for q in probs}
    u0 = {q: _dot(inv_bf[q], p1[q].astype(BF16)) for q in probs}
    w_m = {q: _dot(inv_bf[q], a_bf[q]).astype(BF16) for q in probs}

    state = [state_ref[h] for h in heads]
    y = {}
    for c in chunks:
        s_bf = [state[h].astype(BF16) for h in heads]
        u = [lax.dot_general(w_m[c, h], s_bf[h], _NT, preferred_element_type=F32) + u0[c, h] for h in heads]
        uv = [jnp.concatenate([u[h].astype(BF16), v_bf[c, h]], axis=0) for h in heads]
        for h in heads:
            y[c, h] = (lax.dot_general(r_bf[c, h], s_bf[h], _NT, preferred_element_type=F32)
                       + _dot(gy[c, h], uv[h]))
        state = [state[h] * pro[c]["g_end"][:, sls[h]]
                 + lax.dot_general(uv[h], bk_end[c, h], _TN, preferred_element_type=F32) for h in heads]
    for h in heads:
        state_ref[h] = state[h]

    for c in chunks:
        ys = []
        for h in heads:
            sl = sls[h]
            mean = jnp.mean(y[c, h], axis=-1, keepdims=True)
            yc = y[c, h] - mean
            var = jnp.mean(yc * yc, axis=-1, keepdims=True)
            bonus = jnp.sum(pro[c]["rk"][:, sl], axis=-1, keepdims=True) * pro[c]["v"][:, sl]
            ys.append(yc * lax.rsqrt(var + GN_EPS) * lnw_ref[:, sl] + lnb_ref[:, sl] + bonus)
        o_ref[c * C:(c + 1) * C, :] = (jnp.concatenate(ys, axis=1) * pro[c]["g"]).astype(o_ref.dtype)


def _rwkv(ur, mu, w0, a0, k_k, k_a, r_k, lnw, lnb, wwa, g2, t_blk):
    bsz, seq, cols = ur.shape
    width = w0.shape[1]
    n_heads = width // RWKV_HEAD
    assert cols == 3 * width + 256 and wwa.shape == (128, 2 * width) and g2.shape == (128, width)
    kern = functools.partial(_rwkv_kernel, width=width, n_heads=n_heads, n_chunks=t_blk // RWKV_CHUNK)
    row_w = _const_spec((1, width))
    return pl.pallas_call(
        kern,
        grid=(bsz, seq // t_blk),
        in_specs=[pl.BlockSpec((None, t_blk, cols), lambda b, i: (b, i, 0)),
                  _const_spec((1, cols)), row_w, row_w, row_w, row_w, row_w, row_w, row_w,
                  _const_spec(wwa.shape), _const_spec(g2.shape)],
        out_specs=pl.BlockSpec((None, t_blk, width), lambda b, i: (b, i, 0)),
        out_shape=jax.ShapeDtypeStruct((bsz, seq, width), BF16),
        scratch_shapes=[pltpu.VMEM((8, cols), F32),
                        pltpu.VMEM((n_heads, RWKV_HEAD, RWKV_HEAD), F32)],
        compiler_params=_params("parallel", "arbitrary"),
        name="rwkv7",
    )(ur, mu, w0, a0, k_k, k_a, r_k, lnw, lnb, wwa, g2)


def _attn_kernel(slope_ref, q_ref, k_ref, v_ref, lq1_ref, lk1_ref, lq2_ref, lk2_ref, sg_ref, o_ref,
                 vt_ref, bias_ref, qqt_ref, st0_ref, st1_ref, pt0_ref, pt1_ref, acc_ref, m_ref, l_ref,
                 alpha_ref, *, tq, lambda_init):
    i = pl.program_id(2)
    slope = slope_ref[pl.program_id(1)]
    n_kv = v_ref.shape[0] // tq
    st_refs = (st0_ref, st1_ref)
    pt_refs = (pt0_ref, pt1_ref)

    @pl.when(i == 0)
    def _():
        for c in range(n_kv):
            vt_ref[:, c * tq:(c + 1) * tq] = v_ref[c * tq:(c + 1) * tq, :].astype(F32).T.astype(BF16)
        bias_ref[...] = slope * lax.broadcasted_iota(jnp.int32, bias_ref.shape, 0).astype(F32)

    q = q_ref[...].astype(F32)
    lane = lax.broadcasted_iota(jnp.int32, q.shape, 1)
    half = q.shape[1] // 2
    qq = jnp.concatenate([jnp.where(lane < half, q, 0.0), jnp.where(lane >= half, q, 0.0)], axis=0)
    qqt_ref[...] = qq.T.astype(BF16)

    def scores(j):
        k0 = pl.multiple_of(j * tq, tq)
        return _dot(k_ref[pl.ds(k0, tq), :], qqt_ref[...]) + bias_ref[...]

    def values_t(j):
        return vt_ref[:, pl.ds(pl.multiple_of(j * tq, tq), tq)]

    def step(j, cur, masked):
        nxt = 1 - cur
        if not masked:
            st_refs[nxt][...] = scores(j + 1)
        acc_ref[...] = alpha_ref[...] * acc_ref[...] + _dot(values_t(jnp.maximum(j - 1, 0)), pt_refs[nxt][...])
        st = st_refs[cur][...]
        if masked:
            kpos = lax.broadcasted_iota(jnp.int32, st.shape, 0)
            qpos = lax.broadcasted_iota(jnp.int32, st.shape, 1) % tq
            st = jnp.where(kpos <= qpos, st, MASK_VALUE)
        c_j = slope * (j * tq).astype(F32)
        m = m_ref[...]
        m_new = jnp.maximum(m, jnp.max(st, axis=0, keepdims=True) + c_j)
        alpha = jnp.exp(m - m_new)
        pt = jnp.exp(st - (m_new - c_j))
        l_ref[...] = alpha * l_ref[...] + jnp.sum(pt, axis=0, keepdims=True)
        m_ref[...] = m_new
        alpha_ref[...] = alpha
        pt_refs[cur][...] = pt.astype(BF16)

    st0_ref[...] = scores(0)
    pt1_ref[...] = jnp.zeros_like(pt1_ref)
    acc_ref[...] = jnp.zeros_like(acc_ref)
    m_ref[...] = jnp.full_like(m_ref, MASK_VALUE)
    l_ref[...] = jnp.zeros_like(l_ref)
    alpha_ref[...] = jnp.ones_like(alpha_ref)

    def pair(p, _):
        step(2 * p, 0, False)
        step(2 * p + 1, 1, False)
        return 0

    lax.fori_loop(0, i // 2, pair, 0)
    odd = i % 2 == 1

    @pl.when(odd)
    def _():
        step(i - 1, 0, False)

    def finish(cur):
        step(i, cur, True)
        acc = alpha_ref[...] * acc_ref[...] + _dot(values_t(i), pt_refs[cur][...])
        ot = acc / l_ref[...]
        lam = (jnp.exp(jnp.sum(lq1_ref[...] * lk1_ref[...], axis=-1, keepdims=True))
               - jnp.exp(jnp.sum(lq2_ref[...] * lk2_ref[...], axis=-1, keepdims=True)) + lambda_init)
        o = (ot[:, :tq] - lam * ot[:, tq:]).T
        o = _rms(o, sg_ref[...], SUBLN_EPS) * (1.0 - lambda_init)
        o_ref[...] = o.astype(o_ref.dtype)

    @pl.when(jnp.logical_not(odd))
    def _():
        finish(0)

    @pl.when(odd)
    def _():
        finish(1)


def _attn(ua, slopes, lq1, lk1, lq2, lk2, subln_g, lambda_init, tq):
    bsz, seq, n_a = ua.shape
    d_attn = n_a // 3
    hw = d_attn // DIFF_HEADS
    kern = functools.partial(_attn_kernel, tq=tq, lambda_init=lambda_init)
    lam_spec = _const_spec(lq1.shape)
    return pl.pallas_call(
        kern,
        grid=(bsz, DIFF_HEADS, seq // tq),
        in_specs=[pl.BlockSpec(memory_space=pltpu.SMEM),
                  pl.BlockSpec((None, tq, hw), lambda b, h, i: (b, i, h)),
                  pl.BlockSpec((None, seq, hw), lambda b, h, i: (b, 0, DIFF_HEADS + h)),
                  pl.BlockSpec((None, seq, hw), lambda b, h, i: (b, 0, 2 * DIFF_HEADS + h)),
                  lam_spec, lam_spec, lam_spec, lam_spec, _const_spec(subln_g.shape)],
        out_specs=pl.BlockSpec((None, tq, hw), lambda b, h, i: (b, i, h)),
        out_shape=jax.ShapeDtypeStruct((bsz, seq, d_attn), BF16),
        scratch_shapes=[pltpu.VMEM((hw, seq), BF16),
                        pltpu.VMEM((tq, 2 * tq), F32),
                        pltpu.VMEM((hw, 2 * tq), BF16),
                        pltpu.VMEM((tq, 2 * tq), F32), pltpu.VMEM((tq, 2 * tq), F32),
                        pltpu.VMEM((tq, 2 * tq), BF16), pltpu.VMEM((tq, 2 * tq), BF16),
                        pltpu.VMEM((hw, 2 * tq), F32),
                        pltpu.VMEM((1, 2 * tq), F32), pltpu.VMEM((1, 2 * tq), F32),
                        pltpu.VMEM((1, 2 * tq), F32)],
        compiler_params=_params("parallel", "parallel", "arbitrary"),
        name="diff_attn",
    )(slopes, ua, ua, ua, lq1, lk1, lq2, lk2, subln_g)


def _merge_kernel(x_ref, oa_ref, ob_ref, ug_ref, pa_ref, pb_ref, wo_ref, o_ref):
    d = x_ref.shape[1]
    merged = (ug_ref[:, :d] * _dot(oa_ref[...], pa_ref[...])
              + ug_ref[:, d:] * _dot(ob_ref[...], pb_ref[...]))
    o_ref[...] = x_ref[...] + _dot(merged.astype(BF16), wo_ref[...])


def _merge(x, oa, ob, ug, pa, pb, wo, tm):
    n, d = x.shape
    blk = lambda a: pl.BlockSpec((tm, a.shape[1]), lambda i: (i, 0))
    return pl.pallas_call(
        _merge_kernel,
        grid=(n // tm,),
        in_specs=[blk(x), blk(oa), blk(ob), blk(ug),
                  _const_spec(pa.shape), _const_spec(pb.shape), _const_spec(wo.shape)],
        out_specs=blk(x),
        out_shape=jax.ShapeDtypeStruct((n, d), F32),
        compiler_params=_params("parallel"),
        name="merge",
    )(x, oa, ob, ug, pa, pb, wo)


def _mlp_kernel(x_ref, g_ref, w1_ref, w2_ref, o_ref, *, ff_blk):
    x = x_ref[...]
    h = _rms(x, g_ref[...], NORM_EPS).astype(BF16)
    acc = x
    for c in range(w1_ref.shape[1] // ff_blk):
        a = jnp.maximum(_dot(h, w1_ref[:, c * ff_blk:(c + 1) * ff_blk]), 0.0)
        acc = acc + _dot((a * a).astype(BF16), w2_ref[c * ff_blk:(c + 1) * ff_blk, :])
    o_ref[...] = acc


def _mlp(x, g, w1, w2, tm, ff_blk):
    n, d = x.shape
    return pl.pallas_call(
        functools.partial(_mlp_kernel, ff_blk=ff_blk),
        grid=(n // tm,),
        in_specs=[pl.BlockSpec((tm, d), lambda i: (i, 0)), _const_spec((1, d)),
                  _const_spec(w1.shape), _const_spec(w2.shape)],
        out_specs=pl.BlockSpec((tm, d), lambda i: (i, 0)),
        out_shape=jax.ShapeDtypeStruct((n, d), F32),
        compiler_params=_params("parallel"),
        name="mlp",
    )(x, g, w1, w2)


def _ple_kernel(x_ref, p_ref, g_ref, wp_ref, wg_ref, fg_ref, o_ref, *, final_norm):
    x = x_ref[...]
    gate = _sigmoid(_dot(_rms(x, g_ref[...], NORM_EPS).astype(BF16), wg_ref[...]))
    x = x + _dot(p_ref[...].astype(BF16), wp_ref[...]) * gate
    if final_norm:
        x = _rms(x, fg_ref[...], NORM_EPS)
    o_ref[...] = x


def _ple(x, p, g, wp, wg, fg, final_norm, tm):
    n, d = x.shape
    return pl.pallas_call(
        functools.partial(_ple_kernel, final_norm=final_norm),
        grid=(n // tm,),
        in_specs=[pl.BlockSpec((tm, d), lambda i: (i, 0)), pl.BlockSpec((tm, p.shape[1]), lambda i: (i, 0)),
                  _const_spec((1, d)), _const_spec(wp.shape), _const_spec(wg.shape), _const_spec((1, d))],
        out_specs=pl.BlockSpec((tm, d), lambda i: (i, 0)),
        out_shape=jax.ShapeDtypeStruct((n, d), F32),
        compiler_params=_params("parallel"),
        name="ple",
    )(x, p, g, wp, wg, fg)


def _pick(n, pref):
    while n % pref:
        pref //= 2
    return pref


def kernel(x, p, norm_mix_g, w_in, rwkv_mu, rwkv_w0, rwkv_w2, rwkv_a0, rwkv_a2, rwkv_g2, rwkv_k_k, rwkv_k_a, rwkv_r_k, rwkv_lnx_w, rwkv_lnx_b, lam_q1, lam_k1, lam_q2, lam_k2, diff_subln_g, w_proj_a, w_proj_b, w_out, norm_mlp_g, w_ff1, w_ff2, norm_ple_g, w_ple, w_ple_gate, final_norm_g):
    bsz, seq, d = x.shape
    depth = w_in.shape[0]
    n = bsz * seq
    width = rwkv_w0.shape[1]
    n_r = rwkv_mu.shape[1]
    n_a = 3 * w_proj_b.shape[1]
    d_lora = rwkv_w2.shape[1]
    assert d_lora == 64 and rwkv_a2.shape[1] == 64 and rwkv_g2.shape[1] == 128

    tm = _pick(n, 512)
    tm_in = _pick(n, 256)
    t_blk = _pick(seq, 4 * RWKV_CHUNK)
    tq = _pick(seq, 256)
    ff_blk = _pick(w_ff1.shape[2], 1024)

    row = lambda a: a.reshape(1, -1).astype(F32)
    slopes = 2.0 ** (-8.0 * jnp.arange(1, DIFF_HEADS + 1, dtype=F32) / DIFF_HEADS)
    xf = x.reshape(n, d)
    for i in range(depth):
        wwa = jnp.zeros((128, 2 * width), F32)
        wwa = wwa.at[:64, :width].set(rwkv_w2[i]).at[64:, width:].set(rwkv_a2[i]).astype(BF16)
        ur, ua, ug = _in_proj(xf, row(norm_mix_g[i]), w_in[i].astype(BF16), n_r, n_a, tm_in)
        oa = _rwkv(ur.reshape(bsz, seq, n_r), row(rwkv_mu[i]), row(rwkv_w0[i]), row(rwkv_a0[i]),
                   row(rwkv_k_k[i]), row(rwkv_k_a[i]), row(rwkv_r_k[i]), row(rwkv_lnx_w[i]),
                   row(rwkv_lnx_b[i]), wwa, rwkv_g2[i].astype(BF16), t_blk)
        lambda_init = 0.8 - 0.6 * math.exp(-0.3 * i)
        ob = _attn(ua.reshape(bsz, seq, n_a), slopes, row(lam_q1[i]), row(lam_k1[i]), row(lam_q2[i]),
                   row(lam_k2[i]), row(diff_subln_g[i]), lambda_init, tq)
        xf = _merge(xf, oa.reshape(n, width), ob.reshape(n, n_a // 3), ug,
                    w_proj_a[i].astype(BF16), w_proj_b[i].astype(BF16), w_out[i].astype(BF16), tm)
        xf = _mlp(xf, row(norm_mlp_g[i]), w_ff1[i].astype(BF16), w_ff2[i].astype(BF16), tm, ff_blk)
        xf = _ple(xf, p[i].reshape(n, -1), row(norm_ple_g[i]), w_ple[i].astype(BF16),
                  w_ple_gate[i].astype(BF16), row(final_norm_g), i == depth - 1, tm)
    return xf.reshape(bsz, seq, d)
```

```python
import functools
import math

import jax
import jax.numpy as jnp
from jax import lax
from jax.experimental import pallas as pl
from jax.experimental.pallas import tpu as pltpu

F32 = jnp.float32
BF16 = jnp.bfloat16

NORM_EPS = 1e-6
SUBLN_EPS = 1e-5
RWKV_HEAD = 64
GN_EPS = RWKV_HEAD * 1e-5
DIFF_HEADS = 4
Q_BLOCK = 128

VMEM_LIMIT_BYTES = 52 * 1024 * 1024
MASK_VALUE = -1e30

_NT = (((1,), (1,)), ((), ()))
_TN = (((0,), (0,)), ((), ()))


def _params(*sem):
    return pltpu.CompilerParams(dimension_semantics=sem, vmem_limit_bytes=VMEM_LIMIT_BYTES)


def _const_spec(shape):
    zeros = (0,) * len(shape)
    return pl.BlockSpec(shape, lambda *_: zeros, pipeline_mode=pl.Buffered(1))


def _layer_spec(stacked, layer):
    idx = (layer,) + (0,) * (stacked.ndim - 1)
    return pl.BlockSpec((None,) + stacked.shape[1:], lambda *_: idx, pipeline_mode=pl.Buffered(1))


def _rms(x, g, eps):
    return x * lax.rsqrt(jnp.mean(x * x, axis=-1, keepdims=True) + eps) * g


def _sigmoid(x):
    return 1.0 / (1.0 + jnp.exp(-x))


def _dot(a, b):
    return jnp.dot(a, b, preferred_element_type=F32)


def _in_proj_kernel(x_ref, g_ref, w_ref, ur_ref, ua_ref, ug_ref, *, n_r, n_a, d_attn, q_scale):
    h = _rms(x_ref[...], g_ref[...], NORM_EPS).astype(BF16)
    ur_ref[...] = _dot(h, w_ref[:, :n_r])
    ua_ref[:, :d_attn] = (_dot(h, w_ref[:, n_r:n_r + d_attn]) * q_scale).astype(BF16)
    ua_ref[:, d_attn:] = _dot(h, w_ref[:, n_r + d_attn:n_r + n_a]).astype(BF16)
    ug_ref[...] = _sigmoid(_dot(h, w_ref[:, n_r + n_a:])).astype(ug_ref.dtype)


def _in_proj(x, g, w, layer, n_r, n_a, tm):
    n, d = x.shape
    n_g = w.shape[2] - n_r - n_a
    d_attn = n_a // 3
    q_scale = float((d_attn // DIFF_HEADS // 2) ** -0.5)
    kern = functools.partial(_in_proj_kernel, n_r=n_r, n_a=n_a, d_attn=d_attn, q_scale=q_scale)
    return pl.pallas_call(
        kern,
        grid=(n // tm,),
        in_specs=[pl.BlockSpec((tm, d), lambda i: (i, 0)),
                  _layer_spec(g, layer),
                  _layer_spec(w, layer)],
        out_specs=[pl.BlockSpec((tm, n_r), lambda i: (i, 0)),
                   pl.BlockSpec((tm, n_a), lambda i: (i, 0)),
                   pl.BlockSpec((tm, n_g), lambda i: (i, 0))],
        out_shape=[jax.ShapeDtypeStruct((n, n_r), F32),
                   jax.ShapeDtypeStruct((n, n_a), BF16),
                   jax.ShapeDtypeStruct((n, n_g), BF16)],
        compiler_params=_params("parallel"),
        name="in_proj",
    )(x, g, w)


RWKV_CHUNK = 64


def _split_dot(tri, x):
    hi = x.astype(BF16)
    r1 = x - hi.astype(F32)
    mid = r1.astype(BF16)
    lo = (r1 - mid.astype(F32)).astype(BF16)
    return _dot(tri, hi) + _dot(tri, mid) + _dot(tri, lo)


def _rwkv_kernel(u_ref, mu_ref, w0_ref, a0_ref, kk_ref, ka_ref, rk_ref, lnw_ref, lnb_ref,
                 wwa_ref, g2_ref, o_ref, carry_ref, state_ref, *, width, n_heads, n_chunks):
    C = RWKV_CHUNK
    N = RWKV_HEAD

    @pl.when(pl.program_id(1) == 0)
    def _():
        carry_ref[...] = jnp.zeros_like(carry_ref)
        state_ref[...] = jnp.zeros_like(state_ref)

    row = lax.broadcasted_iota(jnp.int32, (C, 1), 0)
    ti = lax.broadcasted_iota(jnp.int32, (C, C), 0)
    si = lax.broadcasted_iota(jnp.int32, (C, C), 1)
    tri = (si <= ti).astype(BF16)
    eye = (si == ti).astype(F32)
    t2 = lax.broadcasted_iota(jnp.int32, (C, 2 * C), 0)
    s2 = lax.broadcasted_iota(jnp.int32, (C, 2 * C), 1) % C
    lane128 = lax.broadcasted_iota(jnp.int32, (C, 128), 1)
    zeros_cn = jnp.zeros((C, N), BF16)

    heads = range(n_heads)
    chunks = range(n_chunks)
    sls = [slice(h * N, (h + 1) * N) for h in heads]


    pro = []
    for c in chunks:
        x = u_ref[c * C:(c + 1) * C, :]
        first = carry_ref[0:1, :] if c == 0 else u_ref[c * C - 1:c * C, :]
        prev = jnp.where(row == 0, first, pltpu.roll(x, 1, axis=0))
        x = x + (prev - x) * mu_ref[...]
        r = x[:, :width]
        k = x[:, width:2 * width]
        v = x[:, 2 * width:3 * width]
        wa = x[:, 3 * width:3 * width + 128]
        gd = x[:, 3 * width + 128:]
        wa = jnp.where(lane128 < 64, jnp.tanh(wa), wa)
        lora = _dot(wa.astype(BF16), wwa_ref[...])
        z = w0_ref[...] + lora[:, :width]
        w_log = jnp.minimum(z, 0.0) - jnp.log(1.0 + jnp.exp(-jnp.abs(z))) - 0.5
        logw = -jnp.exp(w_log)
        a_sig = _sigmoid(a0_ref[...] + lora[:, width:])
        k2 = k * (1.0 + (a_sig - 1.0) * ka_ref[...])
        l_inc = _split_dot(tri, logw)
        l_end = l_inc[C - 1:C, :]
        e_inv = jnp.exp(-l_inc)
        d_end = jnp.exp(l_end - l_inc)
        pro.append(dict(
            v=v, a_sig=a_sig, e_inv=e_inv, d_end=d_end,
            g=_dot(_sigmoid(gd).astype(BF16), g2_ref[...]),
            kk=k * kk_ref[...], rk=r * k2 * rk_ref[...],
            e_exc=jnp.exp(l_inc - logw), r_t=r * jnp.exp(l_inc),
            g_end=jnp.exp(l_end),
            k_t=k2 * e_inv, k_h=k2 * d_end))
    carry_ref[0:1, :] = u_ref[n_chunks * C - 1:n_chunks * C, :]

    a_bf, r_bf, v_bf, bk_end, ga, gy = ({} for _ in range(6))
    probs = [(c, h) for c in chunks for h in heads]
    for c, h in probs:
        p, sl = pro[c], sls[h]
        kk_n = p["kk"][:, sl]
        kk_n = kk_n / jnp.maximum(jnp.sqrt(jnp.sum(kk_n * kk_n, axis=-1, keepdims=True)), 1e-12)
        b_vec = kk_n * p["a_sig"][:, sl]
        a_bf[c, h] = (-kk_n * p["e_exc"][:, sl]).astype(BF16)
        r_bf[c, h] = p["r_t"][:, sl].astype(BF16)
        v_bf[c, h] = p["v"][:, sl].astype(BF16)
        bk_end[c, h] = jnp.concatenate([(b_vec * p["d_end"][:, sl]).astype(BF16),
                                        p["k_h"][:, sl].astype(BF16)], axis=0)
        ar = jnp.concatenate([a_bf[c, h], r_bf[c, h]], axis=0)
        bk = jnp.concatenate([(b_vec * p["e_inv"][:, sl]).astype(BF16), p["k_t"][:, sl].astype(BF16)], axis=0)
        gm = lax.dot_general(ar, bk, _NT, preferred_element_type=F32)
        ga[c, h] = jnp.where(s2 < t2, gm[:C], 0.0)
        gy[c, h] = jnp.where(s2 <= t2, gm[C:], 0.0).astype(BF16)

    pw = {q: ga[q][:, :C] for q in probs}
    inv = {q: eye + pw[q] for q in probs}
    p1 = {q: _dot(ga[q].astype(BF16), jnp.concatenate([zeros_cn, v_bf[q]], axis=0)) for q in probs}
    for _ in range(int(math.log2(C)) - 1):
        pw_bf = {q: pw[q].astype(BF16) for q in probs}
        pw = {q: _dot(pw_bf[q], pw_bf[q]) for q in probs}
        inv = {q: inv[q] + _dot(inv[q].astype(BF16), pw[q].astype(BF16)) for q in probs}
    inv_bf = {q: inv[q].astype(BF16) for q in probs}
    u0 = {q: _dot(inv_bf[q], p1[q].astype(BF16)) for q in probs}
    w_m = {q: _dot(inv_bf[q], a_bf[q]).astype(BF16) for q in probs}

    state = [state_ref[h] for h in heads]
    y = {}
    for c in chunks:
        s_bf = [state[h].astype(BF16) for h in heads]
        u = [lax.dot_general(w_m[c, h], s_bf[h], _NT, preferred_element_type=F32) + u0[c, h] for h in heads]
        uv = [jnp.concatenate([u[h].astype(BF16), v_bf[c, h]], axis=0) for h in heads]
        for h in heads:
            y[c, h] = (lax.dot_general(r_bf[c, h], s_bf[h], _NT, preferred_element_type=F32)
                       + _dot(gy[c, h], uv[h]))
        state = [state[h] * pro[c]["g_end"][:, sls[h]]
                 + lax.dot_general(uv[h], bk_end[c, h], _TN, preferred_element_type=F32) for h in heads]
    for h in heads:
        state_ref[h] = state[h]

    for c in chunks:
        ys = []
        for h in heads:
            sl = sls[h]
            mean = jnp.mean(y[c, h], axis=-1, keepdims=True)
            yc = y[c, h] - mean
            var = jnp.mean(yc * yc, axis=-1, keepdims=True)
            bonus = jnp.sum(pro[c]["rk"][:, sl], axis=-1, keepdims=True) * pro[c]["v"][:, sl]
            ys.append(yc * lax.rsqrt(var + GN_EPS) * lnw_ref[:, sl] + lnb_ref[:, sl] + bonus)
        o_ref[c * C:(c + 1) * C, :] = (jnp.concatenate(ys, axis=1) * pro[c]["g"]).astype(o_ref.dtype)


def _rwkv(ur, layer, mu, w0, a0, k_k, k_a, r_k, lnw, lnb, wwa, g2, t_blk):
    bsz, seq, cols = ur.shape
    width = w0.shape[2]
    n_heads = width // RWKV_HEAD
    assert cols == 3 * width + 256 and wwa.shape[1:] == (128, 2 * width) and g2.shape[1:] == (128, width)
    kern = functools.partial(_rwkv_kernel, width=width, n_heads=n_heads, n_chunks=t_blk // RWKV_CHUNK)
    params = (mu, w0, a0, k_k, k_a, r_k, lnw, lnb, wwa, g2)
    return pl.pallas_call(
        kern,
        grid=(bsz, seq // t_blk),
        in_specs=[pl.BlockSpec((None, t_blk, cols), lambda b, i: (b, i, 0))]
                 + [_layer_spec(a, layer) for a in params],
        out_specs=pl.BlockSpec((None, t_blk, width), lambda b, i: (b, i, 0)),
        out_shape=jax.ShapeDtypeStruct((bsz, seq, width), BF16),
        scratch_shapes=[pltpu.VMEM((8, cols), F32),
                        pltpu.VMEM((n_heads, RWKV_HEAD, RWKV_HEAD), F32)],
        compiler_params=_params("parallel", "arbitrary"),
        name="rwkv7",
    )(ur, mu, w0, a0, k_k, k_a, r_k, lnw, lnb, wwa, g2)


def _attn_kernel(slope_ref, q_ref, k_ref, v_ref, lq1_ref, lk1_ref, lq2_ref, lk2_ref, sg_ref, o_ref,
                 vt_ref, bias_ref, qqt_ref, st0_ref, st1_ref, pt0_ref, pt1_ref, acc_ref, m_ref, l_ref,
                 alpha_ref, *, tq, lambda_init):
    i = pl.program_id(2)
    slope = slope_ref[pl.program_id(1)]
    n_kv = v_ref.shape[0] // tq
    st_refs = (st0_ref, st1_ref)
    pt_refs = (pt0_ref, pt1_ref)

    @pl.when(i == 0)
    def _():
        for c in range(n_kv):
            vt_ref[:, c * tq:(c + 1) * tq] = v_ref[c * tq:(c + 1) * tq, :].astype(F32).T.astype(BF16)
        bias_ref[...] = slope * lax.broadcasted_iota(jnp.int32, bias_ref.shape, 0).astype(F32)

    q = q_ref[...].astype(F32)
    lane = lax.broadcasted_iota(jnp.int32, q.shape, 1)
    half = q.shape[1] // 2
    qq = jnp.concatenate([jnp.where(lane < half, q, 0.0), jnp.where(lane >= half, q, 0.0)], axis=0)
    qqt_ref[...] = qq.T.astype(BF16)

    def scores(j):
        k0 = pl.multiple_of(j * tq, tq)
        return _dot(k_ref[pl.ds(k0, tq), :], qqt_ref[...]) + bias_ref[...]

    def values_t(j):
        return vt_ref[:, pl.ds(pl.multiple_of(j * tq, tq), tq)]

    def step(j, cur, masked):
        nxt = 1 - cur
        if not masked:
            st_refs[nxt][...] = scores(j + 1)
        acc_ref[...] = alpha_ref[...] * acc_ref[...] + _dot(values_t(jnp.maximum(j - 1, 0)), pt_refs[nxt][...])
        st = st_refs[cur][...]
        if masked:
            kpos = lax.broadcasted_iota(jnp.int32, st.shape, 0)
            qpos = lax.broadcasted_iota(jnp.int32, st.shape, 1) % tq
            st = jnp.where(kpos <= qpos, st, MASK_VALUE)
        c_j = slope * (j * tq).astype(F32)
        m = m_ref[...]
        m_new = jnp.maximum(m, jnp.max(st, axis=0, keepdims=True) + c_j)
        alpha = jnp.exp(m - m_new)
        pt = jnp.exp(st - (m_new - c_j))
        l_ref[...] = alpha * l_ref[...] + jnp.sum(pt, axis=0, keepdims=True)
        m_ref[...] = m_new
        alpha_ref[...] = alpha
        pt_refs[cur][...] = pt.astype(BF16)

    st0_ref[...] = scores(0)
    pt1_ref[...] = jnp.zeros_like(pt1_ref)
    acc_ref[...] = jnp.zeros_like(acc_ref)
    m_ref[...] = jnp.full_like(m_ref, MASK_VALUE)
    l_ref[...] = jnp.zeros_like(l_ref)
    alpha_ref[...] = jnp.ones_like(alpha_ref)

    def pair(p, _):
        step(2 * p, 0, False)
        step(2 * p + 1, 1, False)
        return 0

    lax.fori_loop(0, i // 2, pair, 0)
    odd = i % 2 == 1

    @pl.when(odd)
    def _():
        step(i - 1, 0, False)

    def finish(cur):
        step(i, cur, True)
        acc = alpha_ref[...] * acc_ref[...] + _dot(values_t(i), pt_refs[cur][...])
        ot = acc / l_ref[...]
        lam = (jnp.exp(jnp.sum(lq1_ref[...] * lk1_ref[...], axis=-1, keepdims=True))
               - jnp.exp(jnp.sum(lq2_ref[...] * lk2_ref[...], axis=-1, keepdims=True)) + lambda_init)
        o = (ot[:, :tq] - lam * ot[:, tq:]).T
        o = _rms(o, sg_ref[...], SUBLN_EPS) * (1.0 - lambda_init)
        o_ref[...] = o.astype(o_ref.dtype)

    @pl.when(jnp.logical_not(odd))
    def _():
        finish(0)

    @pl.when(odd)
    def _():
        finish(1)


def _attn(ua, slopes, layer, lq1, lk1, lq2, lk2, subln_g, lambda_init, tq):
    bsz, seq, n_a = ua.shape
    d_attn = n_a // 3
    hw = d_attn // DIFF_HEADS
    kern = functools.partial(_attn_kernel, tq=tq, lambda_init=lambda_init)
    return pl.pallas_call(
        kern,
        grid=(bsz, DIFF_HEADS, seq // tq),
        in_specs=[pl.BlockSpec(memory_space=pltpu.SMEM),
                  pl.BlockSpec((None, tq, hw), lambda b, h, i: (b, i, h)),
                  pl.BlockSpec((None, seq, hw), lambda b, h, i: (b, 0, DIFF_HEADS + h)),
                  pl.BlockSpec((None, seq, hw), lambda b, h, i: (b, 0, 2 * DIFF_HEADS + h))]
                 + [_layer_spec(a, layer) for a in (lq1, lk1, lq2, lk2, subln_g)],
        out_specs=pl.BlockSpec((None, tq, hw), lambda b, h, i: (b, i, h)),
        out_shape=jax.ShapeDtypeStruct((bsz, seq, d_attn), BF16),
        scratch_shapes=[pltpu.VMEM((hw, seq), BF16),
                        pltpu.VMEM((tq, 2 * tq), F32),
                        pltpu.VMEM((hw, 2 * tq), BF16),
                        pltpu.VMEM((tq, 2 * tq), F32), pltpu.VMEM((tq, 2 * tq), F32),
                        pltpu.VMEM((tq, 2 * tq), BF16), pltpu.VMEM((tq, 2 * tq), BF16),
                        pltpu.VMEM((hw, 2 * tq), F32),
                        pltpu.VMEM((1, 2 * tq), F32), pltpu.VMEM((1, 2 * tq), F32),
                        pltpu.VMEM((1, 2 * tq), F32)],
        compiler_params=_params("parallel", "parallel", "arbitrary"),
        name="diff_attn",
    )(slopes, ua, ua, ua, lq1, lk1, lq2, lk2, subln_g)


def _post_kernel(x_ref, oa_ref, ob_ref, ug_ref, p_ref, pa_ref, pb_ref, wo_ref, gm_ref, w1_ref, w2_ref,
                 gp_ref, wp_ref, wg_ref, fg_ref, o_ref, *, ff_blk, final_norm):
    d = x_ref.shape[1]
    merged = (ug_ref[:, :d].astype(F32) * _dot(oa_ref[...], pa_ref[...])
              + ug_ref[:, d:].astype(F32) * _dot(ob_ref[...], pb_ref[...]))
    x = x_ref[...] + _dot(merged.astype(BF16), wo_ref[...])
    h = _rms(x, gm_ref[...], NORM_EPS).astype(BF16)
    acc = x
    for c in range(w1_ref.shape[1] // ff_blk):
        a = jnp.maximum(_dot(h, w1_ref[:, c * ff_blk:(c + 1) * ff_blk]), 0.0)
        acc = acc + _dot((a * a).astype(BF16), w2_ref[c * ff_blk:(c + 1) * ff_blk, :])
    x = acc
    gate = _sigmoid(_dot(_rms(x, gp_ref[...], NORM_EPS).astype(BF16), wg_ref[...]))
    x = x + _dot(p_ref[...].astype(BF16), wp_ref[...]) * gate
    if final_norm:
        x = _rms(x, fg_ref[...], NORM_EPS)
    o_ref[...] = x


def _post(x, oa, ob, ug, p, layer, pa, pb, wo, gm, w1, w2, gp, wp, wg, fg, final_norm, tm, ff_blk):
    n, d = x.shape
    blk = lambda a: pl.BlockSpec((tm, a.shape[1]), lambda i: (i, 0))
    return pl.pallas_call(
        functools.partial(_post_kernel, ff_blk=ff_blk, final_norm=final_norm),
        grid=(n // tm,),
        in_specs=[blk(x), blk(oa), blk(ob), blk(ug),
                  pl.BlockSpec((None, tm, p.shape[2]), lambda i: (layer, i, 0))]
                 + [_layer_spec(a, layer) for a in (pa, pb, wo, gm, w1, w2, gp, wp, wg)]
                 + [_const_spec(fg.shape)],
        out_specs=blk(x),
        out_shape=jax.ShapeDtypeStruct((n, d), F32),
        compiler_params=_params("parallel"),
        name="post",
    )(x, oa, ob, ug, p, pa, pb, wo, gm, w1, w2, gp, wp, wg, fg)


def _pick(n, pref):
    while n % pref:
        pref //= 2
    return pref


def kernel(x, p, norm_mix_g, w_in, rwkv_mu, rwkv_w0, rwkv_w2, rwkv_a0, rwkv_a2, rwkv_g2, rwkv_k_k, rwkv_k_a, rwkv_r_k, rwkv_lnx_w, rwkv_lnx_b, lam_q1, lam_k1, lam_q2, lam_k2, diff_subln_g, w_proj_a, w_proj_b, w_out, norm_mlp_g, w_ff1, w_ff2, norm_ple_g, w_ple, w_ple_gate, final_norm_g):
    bsz, seq, d = x.shape
    depth = w_in.shape[0]
    n = bsz * seq
    width = rwkv_w0.shape[1]
    n_r = rwkv_mu.shape[1]
    n_a = 3 * w_proj_b.shape[1]
    d_lora = rwkv_w2.shape[1]
    assert d_lora == 64 and rwkv_a2.shape[1] == 64 and rwkv_g2.shape[1] == 128

    tm = _pick(n, 512)
    t_blk = _pick(seq, 4 * RWKV_CHUNK)
    tq = _pick(seq, 256)
    ff_blk = _pick(w_ff1.shape[2], 1024)

    bf = lambda a: a.astype(BF16)
    rows = lambda a: a.reshape(depth, 1, -1).astype(F32)
    wwa = jnp.zeros((depth, 128, 2 * width), F32)
    wwa = bf(wwa.at[:, :64, :width].set(rwkv_w2).at[:, 64:, width:].set(rwkv_a2))
    w_in_b, g2_b = bf(w_in), bf(rwkv_g2)
    pa_b, pb_b, wo_b = bf(w_proj_a), bf(w_proj_b), bf(w_out)
    w1_b, w2_b, wp_b, wg_b = bf(w_ff1), bf(w_ff2), bf(w_ple), bf(w_ple_gate)
    g_mix, g_mlp, g_ple = rows(norm_mix_g), rows(norm_mlp_g), rows(norm_ple_g)
    rwkv_rows = [rows(a) for a in (rwkv_mu, rwkv_w0, rwkv_a0, rwkv_k_k, rwkv_k_a, rwkv_r_k,
                                   rwkv_lnx_w, rwkv_lnx_b)]
    attn_rows = [rows(a) for a in (lam_q1, lam_k1, lam_q2, lam_k2, diff_subln_g)]
    fg = final_norm_g.reshape(1, -1).astype(F32)
    p3 = p.reshape(depth, n, -1)
    slopes = 2.0 ** (-8.0 * jnp.arange(1, DIFF_HEADS + 1, dtype=F32) / DIFF_HEADS)

    xf = x.reshape(n, d)
    for i in range(depth):
        ur, ua, ug = _in_proj(xf, g_mix, w_in_b, i, n_r, n_a, tm)
        oa = _rwkv(ur.reshape(bsz, seq, n_r), i, *rwkv_rows, wwa, g2_b, t_blk)
        lambda_init = 0.8 - 0.6 * math.exp(-0.3 * i)
        ob = _attn(ua.reshape(bsz, seq, n_a), slopes, i, *attn_rows, lambda_init, tq)
        xf = _post(xf, oa.reshape(n, width), ob.reshape(n, n_a // 3), ug, p3, i,
                   pa_b, pb_b, wo_b, g_mlp, w1_b, w2_b, g_ple, wp_b, wg_b, fg, i == depth - 1, tm, ff_blk)
    return xf.reshape(bsz, seq, d)
```

```python
import functools
import math

import jax
import jax.numpy as jnp
from jax import lax
from jax.experimental import pallas as pl
from jax.experimental.pallas import tpu as pltpu

F32 = jnp.float32
BF16 = jnp.bfloat16

NORM_EPS = 1e-6
SUBLN_EPS = 1e-5
RWKV_HEAD = 64
GN_EPS = RWKV_HEAD * 1e-5
DIFF_HEADS = 4
Q_BLOCK = 128

VMEM_LIMIT_BYTES = 52 * 1024 * 1024
MASK_VALUE = -1e30
LOG2E = math.log2(math.e)
ATTN_SUM_ROWS = 16

_NT = (((1,), (1,)), ((), ()))
_TN = (((0,), (0,)), ((), ()))


def _params(*sem):
    return pltpu.CompilerParams(dimension_semantics=sem, vmem_limit_bytes=VMEM_LIMIT_BYTES)


def _const_spec(shape):
    zeros = (0,) * len(shape)
    return pl.BlockSpec(shape, lambda *_: zeros, pipeline_mode=pl.Buffered(1))


def _layer_spec(stacked, layer):
    idx = (layer,) + (0,) * (stacked.ndim - 1)
    return pl.BlockSpec((None,) + stacked.shape[1:], lambda *_: idx, pipeline_mode=pl.Buffered(1))


def _rms(x, g, eps):
    return x * lax.rsqrt(jnp.mean(x * x, axis=-1, keepdims=True) + eps) * g


def _sigmoid(x):
    return 1.0 / (1.0 + jnp.exp(-x))


def _dot(a, b):
    return jnp.dot(a, b, preferred_element_type=F32)


def _in_proj_kernel(x_ref, g_ref, w_ref, ur_ref, ua_ref, ug_ref, *, n_r, n_a, d_attn, q_scale):
    h = _rms(x_ref[...], g_ref[...], NORM_EPS).astype(BF16)
    ur_ref[...] = _dot(h, w_ref[:, :n_r])
    ua_ref[:, :d_attn] = (_dot(h, w_ref[:, n_r:n_r + d_attn]) * q_scale).astype(BF16)
    ua_ref[:, d_attn:] = _dot(h, w_ref[:, n_r + d_attn:n_r + n_a]).astype(BF16)
    ug_ref[...] = _sigmoid(_dot(h, w_ref[:, n_r + n_a:])).astype(ug_ref.dtype)


def _in_proj(x, g, w, layer, n_r, n_a, tm):
    n, d = x.shape
    n_g = w.shape[2] - n_r - n_a
    d_attn = n_a // 3
    q_scale = float((d_attn // DIFF_HEADS // 2) ** -0.5) * LOG2E
    kern = functools.partial(_in_proj_kernel, n_r=n_r, n_a=n_a, d_attn=d_attn, q_scale=q_scale)
    return pl.pallas_call(
        kern,
        grid=(n // tm,),
        in_specs=[pl.BlockSpec((tm, d), lambda i: (i, 0)),
                  _layer_spec(g, layer),
                  _layer_spec(w, layer)],
        out_specs=[pl.BlockSpec((tm, n_r), lambda i: (i, 0)),
                   pl.BlockSpec((tm, n_a), lambda i: (i, 0)),
                   pl.BlockSpec((tm, n_g), lambda i: (i, 0))],
        out_shape=[jax.ShapeDtypeStruct((n, n_r), F32),
                   jax.ShapeDtypeStruct((n, n_a), BF16),
                   jax.ShapeDtypeStruct((n, n_g), BF16)],
        compiler_params=_params("parallel"),
        name="in_proj",
    )(x, g, w)


RWKV_CHUNK = 64


def _split_dot(tri, x):
    hi = x.astype(BF16)
    r1 = x - hi.astype(F32)
    mid = r1.astype(BF16)
    lo = (r1 - mid.astype(F32)).astype(BF16)
    return _dot(tri, hi) + _dot(tri, mid) + _dot(tri, lo)


def _rwkv_kernel(u_ref, mu_ref, w0_ref, a0_ref, kk_ref, ka_ref, rk_ref, lnw_ref, lnb_ref,
                 wwa_ref, g2_ref, o_ref, carry_ref, state_ref, *, width, n_heads, n_chunks):
    C = RWKV_CHUNK
    N = RWKV_HEAD

    @pl.when(pl.program_id(1) == 0)
    def _():
        carry_ref[...] = jnp.zeros_like(carry_ref)
        state_ref[...] = jnp.zeros_like(state_ref)

    row = lax.broadcasted_iota(jnp.int32, (C, 1), 0)
    ti = lax.broadcasted_iota(jnp.int32, (C, C), 0)
    si = lax.broadcasted_iota(jnp.int32, (C, C), 1)
    tri = (si <= ti).astype(BF16)
    eye = (si == ti).astype(F32)
    t2 = lax.broadcasted_iota(jnp.int32, (C, 2 * C), 0)
    s2 = lax.broadcasted_iota(jnp.int32, (C, 2 * C), 1) % C
    lane128 = lax.broadcasted_iota(jnp.int32, (C, 128), 1)
    zeros_cn = jnp.zeros((C, N), BF16)

    heads = range(n_heads)
    chunks = range(n_chunks)
    sls = [slice(h * N, (h + 1) * N) for h in heads]


    pro = []
    for c in chunks:
        x = u_ref[c * C:(c + 1) * C, :]
        first = carry_ref[0:1, :] if c == 0 else u_ref[c * C - 1:c * C, :]
        prev = jnp.where(row == 0, first, pltpu.roll(x, 1, axis=0))
        x = x + (prev - x) * mu_ref[...]
        r = x[:, :width]
        k = x[:, width:2 * width]
        v = x[:, 2 * width:3 * width]
        wa = x[:, 3 * width:3 * width + 128]
        gd = x[:, 3 * width + 128:]
        wa = jnp.where(lane128 < 64, jnp.tanh(wa), wa)
        lora = _dot(wa.astype(BF16), wwa_ref[...])
        z = w0_ref[...] + lora[:, :width]
        w_log = jnp.minimum(z, 0.0) - jnp.log(1.0 + jnp.exp(-jnp.abs(z))) - 0.5
        logw = -jnp.exp(w_log)
        a_sig = _sigmoid(a0_ref[...] + lora[:, width:])
        k2 = k * (1.0 + (a_sig - 1.0) * ka_ref[...])
        l_inc = _split_dot(tri, logw)
        l_end = l_inc[C - 1:C, :]
        e_inv = jnp.exp(-l_inc)
        d_end = jnp.exp(l_end - l_inc)
        pro.append(dict(
            v=v, a_sig=a_sig, e_inv=e_inv, d_end=d_end,
            g=_dot(_sigmoid(gd).astype(BF16), g2_ref[...]),
            kk=k * kk_ref[...], rk=r * k2 * rk_ref[...],
            e_exc=jnp.exp(l_inc - logw), r_t=r * jnp.exp(l_inc),
            g_end=jnp.exp(l_end),
            k_t=k2 * e_inv, k_h=k2 * d_end))
    carry_ref[0:1, :] = u_ref[n_chunks * C - 1:n_chunks * C, :]

    a_bf, r_bf, v_bf, bk_end, ga, gy = ({} for _ in range(6))
    probs = [(c, h) for c in chunks for h in heads]
    for c, h in probs:
        p, sl = pro[c], sls[h]
        kk_n = p["kk"][:, sl]
        kk_n = kk_n / jnp.maximum(jnp.sqrt(jnp.sum(kk_n * kk_n, axis=-1, keepdims=True)), 1e-12)
        b_vec = kk_n * p["a_sig"][:, sl]
        a_bf[c, h] = (-kk_n * p["e_exc"][:, sl]).astype(BF16)
        r_bf[c, h] = p["r_t"][:, sl].astype(BF16)
        v_bf[c, h] = p["v"][:, sl].astype(BF16)
        bk_end[c, h] = jnp.concatenate([(b_vec * p["d_end"][:, sl]).astype(BF16),
                                        p["k_h"][:, sl].astype(BF16)], axis=0)
        ar = jnp.concatenate([a_bf[c, h], r_bf[c, h]], axis=0)
        bk = jnp.concatenate([(b_vec * p["e_inv"][:, sl]).astype(BF16), p["k_t"][:, sl].astype(BF16)], axis=0)
        gm = lax.dot_general(ar, bk, _NT, preferred_element_type=F32)
        ga[c, h] = jnp.where(s2 < t2, gm[:C], 0.0)
        gy[c, h] = jnp.where(s2 <= t2, gm[C:], 0.0).astype(BF16)

    pw = {q: ga[q][:, :C] for q in probs}
    inv = {q: eye + pw[q] for q in probs}
    p1 = {q: _dot(ga[q].astype(BF16), jnp.concatenate([zeros_cn, v_bf[q]], axis=0)) for q in probs}
    for _ in range(int(math.log2(C)) - 1):
        pw_bf = {q: pw[q].astype(BF16) for q in probs}
        pw = {q: _dot(pw_bf[q], pw_bf[q]) for q in probs}
        inv = {q: inv[q] + _dot(inv[q].astype(BF16), pw[q].astype(BF16)) for q in probs}
    inv_bf = {q: inv[q].astype(BF16) for q in probs}
    u0 = {q: _dot(inv_bf[q], p1[q].astype(BF16)) for q in probs}
    w_m = {q: _dot(inv_bf[q], a_bf[q]).astype(BF16) for q in probs}

    state = [state_ref[h] for h in heads]
    y = {}
    for c in chunks:
        s_bf = [state[h].astype(BF16) for h in heads]
        u = [lax.dot_general(w_m[c, h], s_bf[h], _NT, preferred_element_type=F32) + u0[c, h] for h in heads]
        uv = [jnp.concatenate([u[h].astype(BF16), v_bf[c, h]], axis=0) for h in heads]
        for h in heads:
            y[c, h] = (lax.dot_general(r_bf[c, h], s_bf[h], _NT, preferred_element_type=F32)
                       + _dot(gy[c, h], uv[h]))
        state = [state[h] * pro[c]["g_end"][:, sls[h]]
                 + lax.dot_general(uv[h], bk_end[c, h], _TN, preferred_element_type=F32) for h in heads]
    for h in heads:
        state_ref[h] = state[h]

    for c in chunks:
        ys = []
        for h in heads:
            sl = sls[h]
            mean = jnp.mean(y[c, h], axis=-1, keepdims=True)
            yc = y[c, h] - mean
            var = jnp.mean(yc * yc, axis=-1, keepdims=True)
            bonus = jnp.sum(pro[c]["rk"][:, sl], axis=-1, keepdims=True) * pro[c]["v"][:, sl]
            ys.append(yc * lax.rsqrt(var + GN_EPS) * lnw_ref[:, sl] + lnb_ref[:, sl] + bonus)
        o_ref[c * C:(c + 1) * C, :] = (jnp.concatenate(ys, axis=1) * pro[c]["g"]).astype(o_ref.dtype)


def _rwkv(ur, layer, mu, w0, a0, k_k, k_a, r_k, lnw, lnb, wwa, g2, t_blk):
    bsz, seq, cols = ur.shape
    width = w0.shape[2]
    n_heads = width // RWKV_HEAD
    assert cols == 3 * width + 256 and wwa.shape[1:] == (128, 2 * width) and g2.shape[1:] == (128, width)
    kern = functools.partial(_rwkv_kernel, width=width, n_heads=n_heads, n_chunks=t_blk // RWKV_CHUNK)
    params = (mu, w0, a0, k_k, k_a, r_k, lnw, lnb, wwa, g2)
    return pl.pallas_call(
        kern,
        grid=(bsz, seq // t_blk),
        in_specs=[pl.BlockSpec((None, t_blk, cols), lambda b, i: (b, i, 0))]
                 + [_layer_spec(a, layer) for a in params],
        out_specs=pl.BlockSpec((None, t_blk, width), lambda b, i: (b, i, 0)),
        out_shape=jax.ShapeDtypeStruct((bsz, seq, width), BF16),
        scratch_shapes=[pltpu.VMEM((8, cols), F32),
                        pltpu.VMEM((n_heads, RWKV_HEAD, RWKV_HEAD), F32)],
        compiler_params=_params("parallel", "arbitrary"),
        name="rwkv7",
    )(ur, mu, w0, a0, k_k, k_a, r_k, lnw, lnb, wwa, g2)


def _attn_kernel(slope_ref, q_ref, k_ref, v_ref, lq1_ref, lk1_ref, lq2_ref, lk2_ref, sg_ref, o_ref,
                 vtw_ref, mask_ref, qqt_ref, st0_ref, st1_ref, pt0_ref, pt1_ref, acc_ref, m_ref, alpha_ref,
                 *, tq, tk, lambda_init):
    i = pl.program_id(2)
    slope = slope_ref[pl.program_id(1)] * LOG2E
    hw = q_ref.shape[1]
    n_kv = v_ref.shape[0] // tk
    st_refs = (st0_ref, st1_ref)
    pt_refs = (pt0_ref, pt1_ref)

    @pl.when(i == 0)
    def _():
        r_sub = lax.broadcasted_iota(jnp.int32, (tk, hw), 0)
        w_tile = jnp.exp2(slope * (r_sub - (tk - 1)).astype(F32))
        for c in range(n_kv):
            vw = v_ref[c * tk:(c + 1) * tk, :].astype(F32) * w_tile
            vtw_ref[:hw, c * tk:(c + 1) * tk] = vw.T.astype(BF16)
        r_lane = lax.broadcasted_iota(jnp.int32, (ATTN_SUM_ROWS, vtw_ref.shape[1]), 1) % tk
        vtw_ref[hw:, :] = jnp.exp2(slope * (r_lane - (tk - 1)).astype(F32)).astype(BF16)
        kpos = lax.broadcasted_iota(jnp.int32, (tk, 2 * tq), 0)
        qpos = lax.broadcasted_iota(jnp.int32, (tk, 2 * tq), 1) % tq
        for d in range(2):
            mask_ref[d] = jnp.where(kpos + d * tk <= qpos, 0.0, MASK_VALUE)

    q = q_ref[...].astype(F32)
    lane = lax.broadcasted_iota(jnp.int32, q.shape, 1)
    half = hw // 2
    qq = jnp.concatenate([jnp.where(lane < half, q, 0.0), jnp.where(lane >= half, q, 0.0)], axis=0)
    qqt_ref[...] = qq.T.astype(BF16)

    def scores(j):
        k0 = pl.multiple_of(j * tk, tk)
        return _dot(k_ref[pl.ds(k0, tk), :], qqt_ref[...])

    def values_t(j):
        return vtw_ref[:, pl.ds(pl.multiple_of(j * tk, tk), tk)]

    def step(j, cur, diag=None, prefetch=True):
        nxt = 1 - cur
        if prefetch:
            st_refs[nxt][...] = scores(j + 1)
        acc_ref[...] = alpha_ref[...] * acc_ref[...] + _dot(values_t(jnp.maximum(j - 1, 0)), pt_refs[nxt][...])
        st = st_refs[cur][...]
        if diag is not None:
            st = st + mask_ref[diag]
        b_max = slope * (j * tk + (tk - 1)).astype(F32)
        m = m_ref[...]
        m_new = jnp.maximum(m, jnp.max(st, axis=0, keepdims=True) + b_max)
        alpha_ref[...] = jnp.exp2(m - m_new)
        m_ref[...] = m_new
        pt_refs[cur][...] = jnp.exp2(st - (m_new - b_max)).astype(BF16)

    st0_ref[...] = scores(0)
    pt1_ref[...] = jnp.zeros_like(pt1_ref)
    acc_ref[...] = jnp.zeros_like(acc_ref)
    m_ref[...] = jnp.full_like(m_ref, MASK_VALUE)
    alpha_ref[...] = jnp.ones_like(alpha_ref)

    def pair(p, _):
        step(2 * p, 0)
        step(2 * p + 1, 1)
        return 0

    lax.fori_loop(0, i, pair, 0)
    step(2 * i, 0, diag=0)
    step(2 * i + 1, 1, diag=1, prefetch=False)
    acc = alpha_ref[...] * acc_ref[...] + _dot(values_t(2 * i + 1), pt1_ref[...])
    ot = acc[:hw] / acc[hw:hw + 1]
    lam = (jnp.exp(jnp.sum(lq1_ref[...] * lk1_ref[...], axis=-1, keepdims=True))
           - jnp.exp(jnp.sum(lq2_ref[...] * lk2_ref[...], axis=-1, keepdims=True)) + lambda_init)
    o = (ot[:, :tq] - lam * ot[:, tq:]).T
    o = _rms(o, sg_ref[...], SUBLN_EPS) * (1.0 - lambda_init)
    o_ref[...] = o.astype(o_ref.dtype)


def _attn(ua, slopes, layer, lq1, lk1, lq2, lk2, subln_g, lambda_init, tq):
    bsz, seq, n_a = ua.shape
    d_attn = n_a // 3
    hw = d_attn // DIFF_HEADS
    tk = tq // 2
    kern = functools.partial(_attn_kernel, tq=tq, tk=tk, lambda_init=lambda_init)
    return pl.pallas_call(
        kern,
        grid=(bsz, DIFF_HEADS, seq // tq),
        in_specs=[pl.BlockSpec(memory_space=pltpu.SMEM),
                  pl.BlockSpec((None, tq, hw), lambda b, h, i: (b, i, h)),
                  pl.BlockSpec((None, seq, hw), lambda b, h, i: (b, 0, DIFF_HEADS + h)),
                  pl.BlockSpec((None, seq, hw), lambda b, h, i: (b, 0, 2 * DIFF_HEADS + h))]
                 + [_layer_spec(a, layer) for a in (lq1, lk1, lq2, lk2, subln_g)],
        out_specs=pl.BlockSpec((None, tq, hw), lambda b, h, i: (b, i, h)),
        out_shape=jax.ShapeDtypeStruct((bsz, seq, d_attn), BF16),
        scratch_shapes=[pltpu.VMEM((hw + ATTN_SUM_ROWS, seq), BF16),
                        pltpu.VMEM((2, tk, 2 * tq), F32),
                        pltpu.VMEM((hw, 2 * tq), BF16),
                        pltpu.VMEM((tk, 2 * tq), F32), pltpu.VMEM((tk, 2 * tq), F32),
                        pltpu.VMEM((tk, 2 * tq), BF16), pltpu.VMEM((tk, 2 * tq), BF16),
                        pltpu.VMEM((hw + ATTN_SUM_ROWS, 2 * tq), F32),
                        pltpu.VMEM((1, 2 * tq), F32), pltpu.VMEM((1, 2 * tq), F32)],
        compiler_params=_params("parallel", "parallel", "arbitrary"),
        name="diff_attn",
    )(slopes, ua, ua, ua, lq1, lk1, lq2, lk2, subln_g)


def _post_kernel(x_ref, oa_ref, ob_ref, ug_ref, p_ref, pa_ref, pb_ref, wo_ref, gm_ref, w1_ref, w2_ref,
                 gp_ref, wp_ref, wg_ref, fg_ref, o_ref, *, ff_blk, final_norm):
    d = x_ref.shape[1]
    merged = (ug_ref[:, :d].astype(F32) * _dot(oa_ref[...], pa_ref[...])
              + ug_ref[:, d:].astype(F32) * _dot(ob_ref[...], pb_ref[...]))
    x = x_ref[...] + _dot(merged.astype(BF16), wo_ref[...])
    h = _rms(x, gm_ref[...], NORM_EPS).astype(BF16)
    acc = x
    for c in range(w1_ref.shape[1] // ff_blk):
        a = jnp.maximum(_dot(h, w1_ref[:, c * ff_blk:(c + 1) * ff_blk]), 0.0)
        acc = acc + _dot((a * a).astype(BF16), w2_ref[c * ff_blk:(c + 1) * ff_blk, :])
    x = acc
    gate = _sigmoid(_dot(_rms(x, gp_ref[...], NORM_EPS).astype(BF16), wg_ref[...]))
    x = x + _dot(p_ref[...].astype(BF16), wp_ref[...]) * gate
    if final_norm:
        x = _rms(x, fg_ref[...], NORM_EPS)
    o_ref[...] = x


def _post(x, oa, ob, ug, p, layer, pa, pb, wo, gm, w1, w2, gp, wp, wg, fg, final_norm, tm, ff_blk):
    n, d = x.shape
    blk = lambda a: pl.BlockSpec((tm, a.shape[1]), lambda i: (i, 0))
    return pl.pallas_call(
        functools.partial(_post_kernel, ff_blk=ff_blk, final_norm=final_norm),
        grid=(n // tm,),
        in_specs=[blk(x), blk(oa), blk(ob), blk(ug),
                  pl.BlockSpec((None, tm, p.shape[2]), lambda i: (layer, i, 0))]
                 + [_layer_spec(a, layer) for a in (pa, pb, wo, gm, w1, w2, gp, wp, wg)]
                 + [_const_spec(fg.shape)],
        out_specs=blk(x),
        out_shape=jax.ShapeDtypeStruct((n, d), F32),
        compiler_params=_params("parallel"),
        name="post",
    )(x, oa, ob, ug, p, pa, pb, wo, gm, w1, w2, gp, wp, wg, fg)


def _pick(n, pref):
    while n % pref:
        pref //= 2
    return pref


def kernel(x, p, norm_mix_g, w_in, rwkv_mu, rwkv_w0, rwkv_w2, rwkv_a0, rwkv_a2, rwkv_g2, rwkv_k_k, rwkv_k_a, rwkv_r_k, rwkv_lnx_w, rwkv_lnx_b, lam_q1, lam_k1, lam_q2, lam_k2, diff_subln_g, w_proj_a, w_proj_b, w_out, norm_mlp_g, w_ff1, w_ff2, norm_ple_g, w_ple, w_ple_gate, final_norm_g):
    bsz, seq, d = x.shape
    depth = w_in.shape[0]
    n = bsz * seq
    width = rwkv_w0.shape[1]
    n_r = rwkv_mu.shape[1]
    n_a = 3 * w_proj_b.shape[1]
    d_lora = rwkv_w2.shape[1]
    assert d_lora == 64 and rwkv_a2.shape[1] == 64 and rwkv_g2.shape[1] == 128

    tm = _pick(n, 512)
    t_blk = _pick(seq, 4 * RWKV_CHUNK)
    tq = _pick(seq, 512)
    ff_blk = _pick(w_ff1.shape[2], 1024)

    bf = lambda a: a.astype(BF16)
    rows = lambda a: a.reshape(depth, 1, -1).astype(F32)
    wwa = jnp.zeros((depth, 128, 2 * width), F32)
    wwa = bf(wwa.at[:, :64, :width].set(rwkv_w2).at[:, 64:, width:].set(rwkv_a2))
    w_in_b, g2_b = bf(w_in), bf(rwkv_g2)
    pa_b, pb_b, wo_b = bf(w_proj_a), bf(w_proj_b), bf(w_out)
    w1_b, w2_b, wp_b, wg_b = bf(w_ff1), bf(w_ff2), bf(w_ple), bf(w_ple_gate)
    g_mix, g_mlp, g_ple = rows(norm_mix_g), rows(norm_mlp_g), rows(norm_ple_g)
    rwkv_rows = [rows(a) for a in (rwkv_mu, rwkv_w0, rwkv_a0, rwkv_k_k, rwkv_k_a, rwkv_r_k,
                                   rwkv_lnx_w, rwkv_lnx_b)]
    attn_rows = [rows(a) for a in (lam_q1, lam_k1, lam_q2, lam_k2, diff_subln_g)]
    fg = final_norm_g.reshape(1, -1).astype(F32)
    p3 = p.reshape(depth, n, -1)
    slopes = 2.0 ** (-8.0 * jnp.arange(1, DIFF_HEADS + 1, dtype=F32) / DIFF_HEADS)

    xf = x.reshape(n, d)
    for i in range(depth):
        ur, ua, ug = _in_proj(xf, g_mix, w_in_b, i, n_r, n_a, tm)
        oa = _rwkv(ur.reshape(bsz, seq, n_r), i, *rwkv_rows, wwa, g2_b, t_blk)
        lambda_init = 0.8 - 0.6 * math.exp(-0.3 * i)
        ob = _attn(ua.reshape(bsz, seq, n_a), slopes, i, *attn_rows, lambda_init, tq)
        xf = _post(xf, oa.reshape(n, width), ob.reshape(n, n_a // 3), ug, p3, i,
                   pa_b, pb_b, wo_b, g_mlp, w1_b, w2_b, g_ple, wp_b, wg_b, fg, i == depth - 1, tm, ff_blk)
    return xf.reshape(bsz, seq, d)
```

```python
import functools
import math

import jax
import jax.numpy as jnp
from jax import lax
from jax.experimental import pallas as pl
from jax.experimental.pallas import tpu as pltpu

F32 = jnp.float32
BF16 = jnp.bfloat16

NORM_EPS = 1e-6
SUBLN_EPS = 1e-5
RWKV_HEAD = 64
GN_EPS = RWKV_HEAD * 1e-5
DIFF_HEADS = 4
Q_BLOCK = 128

VMEM_LIMIT_BYTES = 52 * 1024 * 1024
MASK_VALUE = -1e30
LOG2E = math.log2(math.e)
ATTN_SUM_ROWS = 16

_NT = (((1,), (1,)), ((), ()))
_TN = (((0,), (0,)), ((), ()))


def _params(*sem):
    return pltpu.CompilerParams(dimension_semantics=sem, vmem_limit_bytes=VMEM_LIMIT_BYTES)


def _const_spec(shape):
    zeros = (0,) * len(shape)
    return pl.BlockSpec(shape, lambda *_: zeros, pipeline_mode=pl.Buffered(1))


def _layer_spec(stacked, layer):
    idx = (layer,) + (0,) * (stacked.ndim - 1)
    return pl.BlockSpec((None,) + stacked.shape[1:], lambda *_: idx, pipeline_mode=pl.Buffered(1))


def _rms(x, g, eps):
    return x * lax.rsqrt(jnp.mean(x * x, axis=-1, keepdims=True) + eps) * g


def _sigmoid(x):
    return 1.0 / (1.0 + jnp.exp(-x))


def _dot(a, b):
    return jnp.dot(a, b, preferred_element_type=F32)


def _in_proj_kernel(x_ref, g_ref, w_ref, ur_ref, ua_ref, ug_ref, *, n_r, n_a, d_attn, q_scale):
    h = _rms(x_ref[...], g_ref[...], NORM_EPS).astype(BF16)
    ur_ref[...] = _dot(h, w_ref[:, :n_r])
    ua_ref[:, :d_attn] = (_dot(h, w_ref[:, n_r:n_r + d_attn]) * q_scale).astype(BF16)
    ua_ref[:, d_attn:] = _dot(h, w_ref[:, n_r + d_attn:n_r + n_a]).astype(BF16)
    ug_ref[...] = _sigmoid(_dot(h, w_ref[:, n_r + n_a:])).astype(ug_ref.dtype)


def _in_proj(x, g, w, layer, n_r, n_a, tm):
    n, d = x.shape
    n_g = w.shape[2] - n_r - n_a
    d_attn = n_a // 3
    q_scale = float((d_attn // DIFF_HEADS // 2) ** -0.5) * LOG2E
    kern = functools.partial(_in_proj_kernel, n_r=n_r, n_a=n_a, d_attn=d_attn, q_scale=q_scale)
    return pl.pallas_call(
        kern,
        grid=(n // tm,),
        in_specs=[pl.BlockSpec((tm, d), lambda i: (i, 0)),
                  _layer_spec(g, layer),
                  _layer_spec(w, layer)],
        out_specs=[pl.BlockSpec((tm, n_r), lambda i: (i, 0)),
                   pl.BlockSpec((tm, n_a), lambda i: (i, 0)),
                   pl.BlockSpec((tm, n_g), lambda i: (i, 0))],
        out_shape=[jax.ShapeDtypeStruct((n, n_r), F32),
                   jax.ShapeDtypeStruct((n, n_a), BF16),
                   jax.ShapeDtypeStruct((n, n_g), BF16)],
        compiler_params=_params("parallel"),
        name="in_proj",
    )(x, g, w)


RWKV_CHUNK = 64
DECAY_SCALE = math.exp(-0.5)


def _split_dot(tri, x):
    hi = x.astype(BF16)
    r1 = x - hi.astype(F32)
    mid = r1.astype(BF16)
    lo = (r1 - mid.astype(F32)).astype(BF16)
    return _dot(tri, hi) + _dot(tri, mid) + _dot(tri, lo)


def _rwkv_kernel(u_ref, mu_ref, w0_ref, a0_ref, kk_ref, ka_ref, rk_ref, lnw_ref, lnb_ref,
                 wwa_ref, g2_ref, o_ref, carry_ref, state_ref, *, width, n_heads, n_chunks):
    C = RWKV_CHUNK
    N = RWKV_HEAD

    @pl.when(pl.program_id(1) == 0)
    def _():
        carry_ref[...] = jnp.zeros_like(carry_ref)
        state_ref[...] = jnp.zeros_like(state_ref)

    row = lax.broadcasted_iota(jnp.int32, (C, 1), 0)
    ti = lax.broadcasted_iota(jnp.int32, (C, C), 0)
    si = lax.broadcasted_iota(jnp.int32, (C, C), 1)
    tri = (si <= ti).astype(BF16)
    lane128 = lax.broadcasted_iota(jnp.int32, (C, 128), 1)
    chunks = range(n_chunks)


    pro = []
    for c in chunks:
        x = u_ref[c * C:(c + 1) * C, :]
        first = carry_ref[0:1, :] if c == 0 else u_ref[c * C - 1:c * C, :]
        prev = jnp.where(row == 0, first, pltpu.roll(x, 1, axis=0))
        x = x + (prev - x) * mu_ref[...]
        r = x[:, :width]
        k = x[:, width:2 * width]
        v = x[:, 2 * width:3 * width]
        wa = x[:, 3 * width:3 * width + 128]
        gd = x[:, 3 * width + 128:]
        wa = jnp.where(lane128 < 64, jnp.tanh(wa), wa)
        lora = _dot(wa.astype(BF16), wwa_ref[...])
        z = w0_ref[...] + lora[:, :width]
        logw = -DECAY_SCALE * _sigmoid(z)
        a_sig = _sigmoid(a0_ref[...] + lora[:, width:])
        k2 = k * (1.0 + (a_sig - 1.0) * ka_ref[...])
        l_inc = _split_dot(tri, logw)
        l_end = l_inc[C - 1:C, :]
        e_inv = jnp.exp(-l_inc)
        d_end = jnp.exp(l_end - l_inc)
        pro.append(dict(
            v=v, a_sig=a_sig, e_inv=e_inv, d_end=d_end,
            g=_dot(_sigmoid(gd).astype(BF16), g2_ref[...]),
            kk=k * kk_ref[...], rk=r * k2 * rk_ref[...],
            e_exc=jnp.exp(l_inc - logw), r_t=r * jnp.exp(l_inc),
            g_end=jnp.exp(l_end),
            k_t=k2 * e_inv, k_h=k2 * d_end))
    carry_ref[0:1, :] = u_ref[n_chunks * C - 1:n_chunks * C, :]

    lo = lane128 < N

    def bd(x):
        zero = jnp.zeros_like(x)
        return jnp.concatenate([jnp.where(lo, x, zero), jnp.where(lo, zero, x)], axis=0)

    def head_sum(x):
        s_a = jnp.sum(jnp.where(lo, x, 0.0), axis=-1, keepdims=True)
        s_b = jnp.sum(jnp.where(lo, 0.0, x), axis=-1, keepdims=True)
        return jnp.where(lo, s_a, s_b)

    pairs = range(n_heads // 2)
    cols = [slice(128 * p, 128 * (p + 1)) for p in pairs]
    probs = [(c, p) for c in chunks for p in pairs]
    s_idx = lane128 % N
    strict = s_idx < row
    causal = s_idx <= row
    eye2 = (s_idx == row).astype(F32)

    a_bf, r_bf, v_bf, bk_end, nm, aak, gy = ({} for _ in range(7))
    for c, p in probs:
        pr, sl = pro[c], cols[p]
        kk_p = pr["kk"][:, sl]
        kk_n = kk_p * lax.rsqrt(jnp.maximum(head_sum(kk_p * kk_p), 1e-24))
        b_vec = kk_n * pr["a_sig"][:, sl]
        a_bf[c, p] = (-kk_n * pr["e_exc"][:, sl]).astype(BF16)
        r_bf[c, p] = pr["r_t"][:, sl].astype(BF16)
        v_bf[c, p] = pr["v"][:, sl].astype(BF16)
        bk_end[c, p] = jnp.concatenate([(b_vec * pr["d_end"][:, sl]).astype(BF16),
                                        pr["k_h"][:, sl].astype(BF16)], axis=0)
        ar = jnp.concatenate([a_bf[c, p], r_bf[c, p]], axis=0)
        bk = jnp.concatenate([bd((b_vec * pr["e_inv"][:, sl]).astype(BF16)),
                              bd(pr["k_t"][:, sl].astype(BF16))], axis=0)
        gm = lax.dot_general(ar, bk, _NT, preferred_element_type=F32)
        nm[c, p] = jnp.where(strict, gm[:C, :128], 0.0)
        aak[c, p] = jnp.where(strict, gm[:C, 128:], 0.0).astype(BF16)
        gy[c, p] = jnp.concatenate([jnp.where(causal, gm[C:, :128], 0.0).astype(BF16),
                                    jnp.where(causal, gm[C:, 128:], 0.0).astype(BF16)], axis=1)

    inv = {q: eye2 + nm[q] for q in probs}
    p1 = {q: _dot(aak[q], bd(v_bf[q])) for q in probs}
    pw = {q: nm[q].astype(BF16) for q in probs}
    pw = {q: _dot(pw[q], bd(pw[q])).astype(BF16) for q in probs}
    for _ in range(int(math.log2(C)) - 2):
        both = {q: _dot(jnp.concatenate([inv[q].astype(BF16), pw[q]], axis=0), bd(pw[q])) for q in probs}
        inv = {q: inv[q] + both[q][:C] for q in probs}
        pw = {q: both[q][C:].astype(BF16) for q in probs}
    inv_bf = {q: (inv[q] + _dot(inv[q].astype(BF16), bd(pw[q]))).astype(BF16) for q in probs}
    uw = {q: _dot(inv_bf[q], jnp.concatenate([bd(p1[q].astype(BF16)), bd(a_bf[q])], axis=1)) for q in probs}
    u0 = {q: uw[q][:, :128] for q in probs}
    w_m = {q: uw[q][:, 128:].astype(BF16) for q in probs}

    state = [state_ref[p] for p in pairs]
    y = {}
    for c in chunks:
        s_bd = [bd(state[p].astype(BF16)) for p in pairs]
        ws = [lax.dot_general(jnp.concatenate([w_m[c, p], r_bf[c, p]], axis=0), s_bd[p], _NT,
                              preferred_element_type=F32) for p in pairs]
        u_bf = [(ws[p][:C] + u0[c, p]).astype(BF16) for p in pairs]
        for p in pairs:
            y[c, p] = ws[p][C:] + _dot(gy[c, p], jnp.concatenate([bd(u_bf[p]), bd(v_bf[c, p])], axis=0))
        upd = [lax.dot_general(jnp.concatenate([u_bf[p], v_bf[c, p]], axis=0), bk_end[c, p], _TN,
                               preferred_element_type=F32) for p in pairs]
        state = [state[p] * pro[c]["g_end"][:, cols[p]] + jnp.where(lo, upd[p][:C], upd[p][C:]) for p in pairs]
    for p in pairs:
        state_ref[p] = state[p]

    for c in chunks:
        ys = []
        for p in pairs:
            sl = cols[p]
            yc = y[c, p] - head_sum(y[c, p]) * (1.0 / N)
            var = head_sum(yc * yc) * (1.0 / N)
            bonus = head_sum(pro[c]["rk"][:, sl]) * pro[c]["v"][:, sl]
            ys.append(yc * lax.rsqrt(var + GN_EPS) * lnw_ref[:, sl] + lnb_ref[:, sl] + bonus)
        o_ref[c * C:(c + 1) * C, :] = (jnp.concatenate(ys, axis=1) * pro[c]["g"]).astype(o_ref.dtype)


def _rwkv(ur, layer, mu, w0, a0, k_k, k_a, r_k, lnw, lnb, wwa, g2, t_blk):
    bsz, seq, cols = ur.shape
    width = w0.shape[2]
    n_heads = width // RWKV_HEAD
    assert cols == 3 * width + 256 and wwa.shape[1:] == (128, 2 * width) and g2.shape[1:] == (128, width)
    kern = functools.partial(_rwkv_kernel, width=width, n_heads=n_heads, n_chunks=t_blk // RWKV_CHUNK)
    params = (mu, w0, a0, k_k, k_a, r_k, lnw, lnb, wwa, g2)
    return pl.pallas_call(
        kern,
        grid=(bsz, seq // t_blk),
        in_specs=[pl.BlockSpec((None, t_blk, cols), lambda b, i: (b, i, 0))]
                 + [_layer_spec(a, layer) for a in params],
        out_specs=pl.BlockSpec((None, t_blk, width), lambda b, i: (b, i, 0)),
        out_shape=jax.ShapeDtypeStruct((bsz, seq, width), BF16),
        scratch_shapes=[pltpu.VMEM((8, cols), F32),
                        pltpu.VMEM((n_heads // 2, RWKV_HEAD, 2 * RWKV_HEAD), F32)],
        compiler_params=_params("parallel", "arbitrary"),
        name="rwkv7",
    )(ur, mu, w0, a0, k_k, k_a, r_k, lnw, lnb, wwa, g2)


def _attn_kernel(slope_ref, q_ref, k_ref, v_ref, lq1_ref, lk1_ref, lq2_ref, lk2_ref, sg_ref, o_ref,
                 vtw_ref, mask_ref, qqt_ref, st0_ref, st1_ref, pt0_ref, pt1_ref, acc_ref, m_ref, alpha_ref,
                 *, tq, tk, lambda_init):
    i = pl.program_id(2)
    slope = slope_ref[pl.program_id(1)] * LOG2E
    hw = q_ref.shape[1]
    n_kv = v_ref.shape[0] // tk
    st_refs = (st0_ref, st1_ref)
    pt_refs = (pt0_ref, pt1_ref)

    @pl.when(i == 0)
    def _():
        r_sub = lax.broadcasted_iota(jnp.int32, (tk, hw), 0)
        w_tile = jnp.exp2(slope * (r_sub - (tk - 1)).astype(F32))
        for c in range(n_kv):
            vw = v_ref[c * tk:(c + 1) * tk, :].astype(F32) * w_tile
            vtw_ref[:hw, c * tk:(c + 1) * tk] = vw.T.astype(BF16)
        r_lane = lax.broadcasted_iota(jnp.int32, (ATTN_SUM_ROWS, vtw_ref.shape[1]), 1) % tk
        vtw_ref[hw:, :] = jnp.exp2(slope * (r_lane - (tk - 1)).astype(F32)).astype(BF16)
        kpos = lax.broadcasted_iota(jnp.int32, (tk, 2 * tq), 0)
        qpos = lax.broadcasted_iota(jnp.int32, (tk, 2 * tq), 1) % tq
        for d in range(2):
            mask_ref[d] = jnp.where(kpos + d * tk <= qpos, 0.0, MASK_VALUE)

    q = q_ref[...].astype(F32)
    lane = lax.broadcasted_iota(jnp.int32, q.shape, 1)
    half = hw // 2
    qq = jnp.concatenate([jnp.where(lane < half, q, 0.0), jnp.where(lane >= half, q, 0.0)], axis=0)
    qqt_ref[...] = qq.T.astype(BF16)

    def scores(j):
        k0 = pl.multiple_of(j * tk, tk)
        return _dot(k_ref[pl.ds(k0, tk), :], qqt_ref[...])

    def values_t(j):
        return vtw_ref[:, pl.ds(pl.multiple_of(j * tk, tk), tk)]

    def step(j, cur, diag=None, prefetch=True):
        nxt = 1 - cur
        if prefetch:
            st_refs[nxt][...] = scores(j + 1)
        acc_ref[...] = alpha_ref[...] * acc_ref[...] + _dot(values_t(jnp.maximum(j - 1, 0)), pt_refs[nxt][...])
        st = st_refs[cur][...]
        if diag is not None:
            st = st + mask_ref[diag]
        b_max = slope * (j * tk + (tk - 1)).astype(F32)
        m = m_ref[...]
        m_new = jnp.maximum(m, jnp.max(st, axis=0, keepdims=True) + b_max)
        alpha_ref[...] = jnp.exp2(m - m_new)
        m_ref[...] = m_new
        pt_refs[cur][...] = jnp.exp2(st - (m_new - b_max)).astype(BF16)

    st0_ref[...] = scores(0)
    pt1_ref[...] = jnp.zeros_like(pt1_ref)
    acc_ref[...] = jnp.zeros_like(acc_ref)
    m_ref[...] = jnp.full_like(m_ref, MASK_VALUE)
    alpha_ref[...] = jnp.ones_like(alpha_ref)

    def pair(p, _):
        step(2 * p, 0)
        step(2 * p + 1, 1)
        return 0

    lax.fori_loop(0, i, pair, 0)
    step(2 * i, 0, diag=0)
    step(2 * i + 1, 1, diag=1, prefetch=False)
    acc = alpha_ref[...] * acc_ref[...] + _dot(values_t(2 * i + 1), pt1_ref[...])
    ot = acc[:hw] / acc[hw:hw + 1]
    lam = (jnp.exp(jnp.sum(lq1_ref[...] * lk1_ref[...], axis=-1, keepdims=True))
           - jnp.exp(jnp.sum(lq2_ref[...] * lk2_ref[...], axis=-1, keepdims=True)) + lambda_init)
    o = (ot[:, :tq] - lam * ot[:, tq:]).T
    o = _rms(o, sg_ref[...], SUBLN_EPS) * (1.0 - lambda_init)
    o_ref[...] = o.astype(o_ref.dtype)


def _attn(ua, slopes, layer, lq1, lk1, lq2, lk2, subln_g, lambda_init, tq):
    bsz, seq, n_a = ua.shape
    d_attn = n_a // 3
    hw = d_attn // DIFF_HEADS
    tk = tq // 2
    kern = functools.partial(_attn_kernel, tq=tq, tk=tk, lambda_init=lambda_init)
    return pl.pallas_call(
        kern,
        grid=(bsz, DIFF_HEADS, seq // tq),
        in_specs=[pl.BlockSpec(memory_space=pltpu.SMEM),
                  pl.BlockSpec((None, tq, hw), lambda b, h, i: (b, i, h)),
                  pl.BlockSpec((None, seq, hw), lambda b, h, i: (b, 0, DIFF_HEADS + h)),
                  pl.BlockSpec((None, seq, hw), lambda b, h, i: (b, 0, 2 * DIFF_HEADS + h))]
                 + [_layer_spec(a, layer) for a in (lq1, lk1, lq2, lk2, subln_g)],
        out_specs=pl.BlockSpec((None, tq, hw), lambda b, h, i: (b, i, h)),
        out_shape=jax.ShapeDtypeStruct((bsz, seq, d_attn), BF16),
        scratch_shapes=[pltpu.VMEM((hw + ATTN_SUM_ROWS, seq), BF16),
                        pltpu.VMEM((2, tk, 2 * tq), F32),
                        pltpu.VMEM((hw, 2 * tq), BF16),
                        pltpu.VMEM((tk, 2 * tq), F32), pltpu.VMEM((tk, 2 * tq), F32),
                        pltpu.VMEM((tk, 2 * tq), BF16), pltpu.VMEM((tk, 2 * tq), BF16),
                        pltpu.VMEM((hw + ATTN_SUM_ROWS, 2 * tq), F32),
                        pltpu.VMEM((1, 2 * tq), F32), pltpu.VMEM((1, 2 * tq), F32)],
        compiler_params=_params("parallel", "parallel", "arbitrary"),
        name="diff_attn",
    )(slopes, ua, ua, ua, lq1, lk1, lq2, lk2, subln_g)


def _post_kernel(x_ref, oa_ref, ob_ref, ug_ref, p_ref, pa_ref, pb_ref, wo_ref, gm_ref, w1_ref, w2_ref,
                 gp_ref, wp_ref, wg_ref, fg_ref, o_ref, *, ff_blk, final_norm):
    d = x_ref.shape[1]
    merged = (ug_ref[:, :d].astype(F32) * _dot(oa_ref[...], pa_ref[...])
              + ug_ref[:, d:].astype(F32) * _dot(ob_ref[...], pb_ref[...]))
    x = x_ref[...] + _dot(merged.astype(BF16), wo_ref[...])
    h = _rms(x, gm_ref[...], NORM_EPS).astype(BF16)
    acc = x
    for c in range(w1_ref.shape[1] // ff_blk):
        a = jnp.maximum(_dot(h, w1_ref[:, c * ff_blk:(c + 1) * ff_blk]), 0.0)
        acc = acc + _dot((a * a).astype(BF16), w2_ref[c * ff_blk:(c + 1) * ff_blk, :])
    x = acc
    gate = _sigmoid(_dot(_rms(x, gp_ref[...], NORM_EPS).astype(BF16), wg_ref[...]))
    x = x + _dot(p_ref[...].astype(BF16), wp_ref[...]) * gate
    if final_norm:
        x = _rms(x, fg_ref[...], NORM_EPS)
    o_ref[...] = x


def _post(x, oa, ob, ug, p, layer, pa, pb, wo, gm, w1, w2, gp, wp, wg, fg, final_norm, tm, ff_blk):
    n, d = x.shape
    blk = lambda a: pl.BlockSpec((tm, a.shape[1]), lambda i: (i, 0))
    return pl.pallas_call(
        functools.partial(_post_kernel, ff_blk=ff_blk, final_norm=final_norm),
        grid=(n // tm,),
        in_specs=[blk(x), blk(oa), blk(ob), blk(ug),
                  pl.BlockSpec((None, tm, p.shape[2]), lambda i: (layer, i, 0))]
                 + [_layer_spec(a, layer) for a in (pa, pb, wo, gm, w1, w2, gp, wp, wg)]
                 + [_const_spec(fg.shape)],
        out_specs=blk(x),
        out_shape=jax.ShapeDtypeStruct((n, d), F32),
        compiler_params=_params("parallel"),
        name="post",
    )(x, oa, ob, ug, p, pa, pb, wo, gm, w1, w2, gp, wp, wg, fg)


def _pick(n, pref):
    while n % pref:
        pref //= 2
    return pref


def kernel(x, p, norm_mix_g, w_in, rwkv_mu, rwkv_w0, rwkv_w2, rwkv_a0, rwkv_a2, rwkv_g2, rwkv_k_k, rwkv_k_a, rwkv_r_k, rwkv_lnx_w, rwkv_lnx_b, lam_q1, lam_k1, lam_q2, lam_k2, diff_subln_g, w_proj_a, w_proj_b, w_out, norm_mlp_g, w_ff1, w_ff2, norm_ple_g, w_ple, w_ple_gate, final_norm_g):
    bsz, seq, d = x.shape
    depth = w_in.shape[0]
    n = bsz * seq
    width = rwkv_w0.shape[1]
    n_r = rwkv_mu.shape[1]
    n_a = 3 * w_proj_b.shape[1]
    d_lora = rwkv_w2.shape[1]
    assert d_lora == 64 and rwkv_a2.shape[1] == 64 and rwkv_g2.shape[1] == 128

    tm = _pick(n, 512)
    t_blk = _pick(seq, 4 * RWKV_CHUNK)
    tq = _pick(seq, 512)
    ff_blk = _pick(w_ff1.shape[2], 1024)

    bf = lambda a: a.astype(BF16)
    rows = lambda a: a.reshape(depth, 1, -1).astype(F32)
    wwa = jnp.zeros((depth, 128, 2 * width), F32)
    wwa = bf(wwa.at[:, :64, :width].set(rwkv_w2).at[:, 64:, width:].set(rwkv_a2))
    w_in_b, g2_b = bf(w_in), bf(rwkv_g2)
    pa_b, pb_b, wo_b = bf(w_proj_a), bf(w_proj_b), bf(w_out)
    w1_b, w2_b, wp_b, wg_b = bf(w_ff1), bf(w_ff2), bf(w_ple), bf(w_ple_gate)
    g_mix, g_mlp, g_ple = rows(norm_mix_g), rows(norm_mlp_g), rows(norm_ple_g)
    rwkv_rows = [rows(a) for a in (rwkv_mu, rwkv_w0, rwkv_a0, rwkv_k_k, rwkv_k_a, rwkv_r_k,
                                   rwkv_lnx_w, rwkv_lnx_b)]
    attn_rows = [rows(a) for a in (lam_q1, lam_k1, lam_q2, lam_k2, diff_subln_g)]
    fg = final_norm_g.reshape(1, -1).astype(F32)
    p3 = p.reshape(depth, n, -1)
    slopes = 2.0 ** (-8.0 * jnp.arange(1, DIFF_HEADS + 1, dtype=F32) / DIFF_HEADS)

    xf = x.reshape(n, d)
    for i in range(depth):
        ur, ua, ug = _in_proj(xf, g_mix, w_in_b, i, n_r, n_a, tm)
        oa = _rwkv(ur.reshape(bsz, seq, n_r), i, *rwkv_rows, wwa, g2_b, t_blk)
        lambda_init = 0.8 - 0.6 * math.exp(-0.3 * i)
        ob = _attn(ua.reshape(bsz, seq, n_a), slopes, i, *attn_rows, lambda_init, tq)
        xf = _post(xf, oa.reshape(n, width), ob.reshape(n, n_a // 3), ug, p3, i,
                   pa_b, pb_b, wo_b, g_mlp, w1_b, w2_b, g_ple, wp_b, wg_b, fg, i == depth - 1, tm, ff_blk)
    return xf.reshape(bsz, seq, d)
```

```python
import functools
import math

import jax
import jax.numpy as jnp
from jax import lax
from jax.experimental import pallas as pl
from jax.experimental.pallas import tpu as pltpu

F32 = jnp.float32
BF16 = jnp.bfloat16

NORM_EPS = 1e-6
SUBLN_EPS = 1e-5
RWKV_HEAD = 64
GN_EPS = RWKV_HEAD * 1e-5
DIFF_HEADS = 4
Q_BLOCK = 128

VMEM_LIMIT_BYTES = 52 * 1024 * 1024
MASK_VALUE = -1e30
LOG2E = math.log2(math.e)
ATTN_SUM_ROWS = 16

_NT = (((1,), (1,)), ((), ()))
_TN = (((0,), (0,)), ((), ()))


def _params(*sem):
    return pltpu.CompilerParams(dimension_semantics=sem, vmem_limit_bytes=VMEM_LIMIT_BYTES)


def _const_spec(shape):
    zeros = (0,) * len(shape)
    return pl.BlockSpec(shape, lambda *_: zeros, pipeline_mode=pl.Buffered(1))


def _layer_spec(stacked, layer):
    idx = (layer,) + (0,) * (stacked.ndim - 1)
    return pl.BlockSpec((None,) + stacked.shape[1:], lambda *_: idx, pipeline_mode=pl.Buffered(1))


def _rms(x, g, eps):
    return x * lax.rsqrt(jnp.mean(x * x, axis=-1, keepdims=True) + eps) * g


def _sigmoid(x):
    return 0.5 * jnp.tanh(0.5 * x) + 0.5


def _dot(a, b):
    return jnp.dot(a, b, preferred_element_type=F32)


def _in_proj_kernel(x_ref, g_ref, w_ref, mu_ref, ur_ref, ua_ref, ug_ref, carry_ref,
                    *, n_r, n_a, d_attn, q_scale, blocks_per_seq):
    i = pl.program_id(0)

    @pl.when(i == 0)
    def _():
        carry_ref[...] = jnp.zeros_like(carry_ref)

    h = _rms(x_ref[...], g_ref[...], NORM_EPS).astype(BF16)
    c = _dot(h, w_ref[:, :n_r])
    row = lax.broadcasted_iota(jnp.int32, (c.shape[0], 1), 0)
    first = jnp.where(i % blocks_per_seq == 0, 0.0, carry_ref[0:1, :])
    prev = jnp.where(row == 0, first, pltpu.roll(c, 1, axis=0))
    carry_ref[0:1, :] = c[c.shape[0] - 1:, :]
    ur_ref[...] = c + (prev - c) * mu_ref[...]
    ua_ref[:, :d_attn] = (_dot(h, w_ref[:, n_r:n_r + d_attn]) * q_scale).astype(BF16)
    ua_ref[:, d_attn:] = _dot(h, w_ref[:, n_r + d_attn:n_r + n_a]).astype(BF16)
    ug_ref[...] = _sigmoid(_dot(h, w_ref[:, n_r + n_a:])).astype(ug_ref.dtype)


def _in_proj(x, g, w, mu, layer, seq, n_r, n_a, tm):
    n, d = x.shape
    n_g = w.shape[2] - n_r - n_a
    d_attn = n_a // 3
    q_scale = float((d_attn // DIFF_HEADS // 2) ** -0.5) * LOG2E
    assert seq % tm == 0
    kern = functools.partial(_in_proj_kernel, n_r=n_r, n_a=n_a, d_attn=d_attn, q_scale=q_scale,
                             blocks_per_seq=seq // tm)
    return pl.pallas_call(
        kern,
        grid=(n // tm,),
        in_specs=[pl.BlockSpec((tm, d), lambda i: (i, 0)),
                  _layer_spec(g, layer),
                  _layer_spec(w, layer),
                  _layer_spec(mu, layer)],
        out_specs=[pl.BlockSpec((tm, n_r), lambda i: (i, 0)),
                   pl.BlockSpec((tm, n_a), lambda i: (i, 0)),
                   pl.BlockSpec((tm, n_g), lambda i: (i, 0))],
        out_shape=[jax.ShapeDtypeStruct((n, n_r), F32),
                   jax.ShapeDtypeStruct((n, n_a), BF16),
                   jax.ShapeDtypeStruct((n, n_g), BF16)],
        scratch_shapes=[pltpu.VMEM((8, n_r), F32)],
        compiler_params=_params("arbitrary"),
        name="in_proj",
    )(x, g, w, mu)


RWKV_CHUNK = 64
DECAY_SCALE = math.exp(-0.5)


def _split_dot(tri, x):
    hi = x.astype(BF16)
    r1 = x - hi.astype(F32)
    mid = r1.astype(BF16)
    lo = (r1 - mid.astype(F32)).astype(BF16)
    return _dot(tri, hi) + _dot(tri, mid) + _dot(tri, lo)


def _rwkv_kernel(u_ref, w0_ref, a0_ref, kk_ref, ka_ref, rk_ref, lnw_ref, lnb_ref,
                 wwa_ref, g2_ref, o_ref, state_ref, *, width, n_heads, n_chunks):
    C = RWKV_CHUNK
    N = RWKV_HEAD

    @pl.when(pl.program_id(1) == 0)
    def _():
        state_ref[...] = jnp.zeros_like(state_ref)

    row = lax.broadcasted_iota(jnp.int32, (C, 1), 0)
    ti = lax.broadcasted_iota(jnp.int32, (C, C), 0)
    si = lax.broadcasted_iota(jnp.int32, (C, C), 1)
    tri = (si <= ti).astype(BF16)
    lane128 = lax.broadcasted_iota(jnp.int32, (C, 128), 1)
    chunks = range(n_chunks)


    pro = []
    for c in chunks:
        x = u_ref[c * C:(c + 1) * C, :]
        r = x[:, :width]
        k = x[:, width:2 * width]
        v = x[:, 2 * width:3 * width]
        wa = x[:, 3 * width:3 * width + 128]
        gd = x[:, 3 * width + 128:]
        wa = jnp.where(lane128 < 64, jnp.tanh(wa), wa)
        lora = _dot(wa.astype(BF16), wwa_ref[...])
        z = w0_ref[...] + lora[:, :width]
        logw = -DECAY_SCALE * _sigmoid(z)
        a_sig = _sigmoid(a0_ref[...] + lora[:, width:])
        k2 = k * (1.0 + (a_sig - 1.0) * ka_ref[...])
        l_inc = _split_dot(tri, logw)
        l_end = l_inc[C - 1:C, :]
        e_inv = jnp.exp(-l_inc)
        d_end = jnp.exp(l_end - l_inc)
        pro.append(dict(
            v=v, a_sig=a_sig, e_inv=e_inv, d_end=d_end,
            g=_dot(_sigmoid(gd).astype(BF16), g2_ref[...]),
            kk=k * kk_ref[...], rk=r * k2 * rk_ref[...],
            e_exc=jnp.exp(l_inc - logw), r_t=r * jnp.exp(l_inc),
            g_end=jnp.exp(l_end),
            k_t=k2 * e_inv, k_h=k2 * d_end))

    lo = lane128 < N

    def bd(x):
        zero = jnp.zeros_like(x)
        return jnp.concatenate([jnp.where(lo, x, zero), jnp.where(lo, zero, x)], axis=0)

    def head_sum(x):
        s_a = jnp.sum(jnp.where(lo, x, 0.0), axis=-1, keepdims=True)
        s_b = jnp.sum(jnp.where(lo, 0.0, x), axis=-1, keepdims=True)
        return jnp.where(lo, s_a, s_b)

    pairs = range(n_heads // 2)
    cols = [slice(128 * p, 128 * (p + 1)) for p in pairs]
    probs = [(c, p) for c in chunks for p in pairs]
    s_idx = lane128 % N
    strict = s_idx < row
    causal = s_idx <= row
    eye2 = (s_idx == row).astype(F32)

    a_bf, r_bf, v_bf, bk_end, nm, aak, gy = ({} for _ in range(7))
    for c, p in probs:
        pr, sl = pro[c], cols[p]
        kk_p = pr["kk"][:, sl]
        kk_n = kk_p * lax.rsqrt(jnp.maximum(head_sum(kk_p * kk_p), 1e-24))
        b_vec = kk_n * pr["a_sig"][:, sl]
        a_bf[c, p] = (-kk_n * pr["e_exc"][:, sl]).astype(BF16)
        r_bf[c, p] = pr["r_t"][:, sl].astype(BF16)
        v_bf[c, p] = pr["v"][:, sl].astype(BF16)
        bk_end[c, p] = jnp.concatenate([(b_vec * pr["d_end"][:, sl]).astype(BF16),
                                        pr["k_h"][:, sl].astype(BF16)], axis=0)
        ar = jnp.concatenate([a_bf[c, p], r_bf[c, p]], axis=0)
        bk = jnp.concatenate([bd((b_vec * pr["e_inv"][:, sl]).astype(BF16)),
                              bd(pr["k_t"][:, sl].astype(BF16))], axis=0)
        gm = lax.dot_general(ar, bk, _NT, preferred_element_type=F32)
        nm[c, p] = jnp.where(strict, gm[:C, :128], 0.0)
        aak[c, p] = jnp.where(strict, gm[:C, 128:], 0.0).astype(BF16)
        gy[c, p] = jnp.concatenate([jnp.where(causal, gm[C:, :128], 0.0).astype(BF16),
                                    jnp.where(causal, gm[C:, 128:], 0.0).astype(BF16)], axis=1)

    inv = {q: eye2 + nm[q] for q in probs}
    p1 = {q: _dot(aak[q], bd(v_bf[q])) for q in probs}
    pw = {q: nm[q].astype(BF16) for q in probs}
    pw = {q: _dot(pw[q], bd(pw[q])).astype(BF16) for q in probs}
    for _ in range(int(math.log2(C)) - 2):
        both = {q: _dot(jnp.concatenate([inv[q].astype(BF16), pw[q]], axis=0), bd(pw[q])) for q in probs}
        inv = {q: inv[q] + both[q][:C] for q in probs}
        pw = {q: both[q][C:].astype(BF16) for q in probs}
    inv_bf = {q: (inv[q] + _dot(inv[q].astype(BF16), bd(pw[q]))).astype(BF16) for q in probs}
    uw = {q: _dot(inv_bf[q], jnp.concatenate([bd(p1[q].astype(BF16)), bd(a_bf[q])], axis=1)) for q in probs}
    u0 = {q: uw[q][:, :128] for q in probs}
    w_m = {q: uw[q][:, 128:].astype(BF16) for q in probs}

    state = [state_ref[p] for p in pairs]
    y = {}
    for c in chunks:
        s_bd = [bd(state[p].astype(BF16)) for p in pairs]
        ws = [lax.dot_general(jnp.concatenate([w_m[c, p], r_bf[c, p]], axis=0), s_bd[p], _NT,
                              preferred_element_type=F32) for p in pairs]
        u_bf = [(ws[p][:C] + u0[c, p]).astype(BF16) for p in pairs]
        for p in pairs:
            y[c, p] = ws[p][C:] + _dot(gy[c, p], jnp.concatenate([bd(u_bf[p]), bd(v_bf[c, p])], axis=0))
        upd = [lax.dot_general(jnp.concatenate([u_bf[p], v_bf[c, p]], axis=0), bk_end[c, p], _TN,
                               preferred_element_type=F32) for p in pairs]
        state = [state[p] * pro[c]["g_end"][:, cols[p]] + jnp.where(lo, upd[p][:C], upd[p][C:]) for p in pairs]
    for p in pairs:
        state_ref[p] = state[p]

    for c in chunks:
        ys = []
        for p in pairs:
            sl = cols[p]
            yc = y[c, p] - head_sum(y[c, p]) * (1.0 / N)
            var = head_sum(yc * yc) * (1.0 / N)
            bonus = head_sum(pro[c]["rk"][:, sl]) * pro[c]["v"][:, sl]
            ys.append(yc * lax.rsqrt(var + GN_EPS) * lnw_ref[:, sl] + lnb_ref[:, sl] + bonus)
        o_ref[c * C:(c + 1) * C, :] = (jnp.concatenate(ys, axis=1) * pro[c]["g"]).astype(o_ref.dtype)


def _rwkv(ur, layer, w0, a0, k_k, k_a, r_k, lnw, lnb, wwa, g2, t_blk):
    bsz, seq, cols = ur.shape
    width = w0.shape[2]
    n_heads = width // RWKV_HEAD
    assert cols == 3 * width + 256 and wwa.shape[1:] == (128, 2 * width) and g2.shape[1:] == (128, width)
    kern = functools.partial(_rwkv_kernel, width=width, n_heads=n_heads, n_chunks=t_blk // RWKV_CHUNK)
    params = (w0, a0, k_k, k_a, r_k, lnw, lnb, wwa, g2)
    return pl.pallas_call(
        kern,
        grid=(bsz, seq // t_blk),
        in_specs=[pl.BlockSpec((None, t_blk, cols), lambda b, i: (b, i, 0))]
                 + [_layer_spec(a, layer) for a in params],
        out_specs=pl.BlockSpec((None, t_blk, width), lambda b, i: (b, i, 0)),
        out_shape=jax.ShapeDtypeStruct((bsz, seq, width), BF16),
        scratch_shapes=[pltpu.VMEM((n_heads // 2, RWKV_HEAD, 2 * RWKV_HEAD), F32)],
        compiler_params=_params("parallel", "arbitrary"),
        name="rwkv7",
    )(ur, *params)


def _attn_kernel(slope_ref, q_ref, k_ref, v_ref, lq1_ref, lk1_ref, lq2_ref, lk2_ref, sg_ref, o_ref,
                 vtw_ref, mask_ref, qqt_ref, st0_ref, st1_ref, pt0_ref, pt1_ref, acc_ref, m_ref, alpha_ref,
                 *, tq, tk, lambda_init):
    i = pl.program_id(2)
    slope = slope_ref[pl.program_id(1)] * LOG2E
    hw = q_ref.shape[1]
    n_kv = v_ref.shape[0] // tk
    st_refs = (st0_ref, st1_ref)
    pt_refs = (pt0_ref, pt1_ref)

    @pl.when(i == 0)
    def _():
        r_sub = lax.broadcasted_iota(jnp.int32, (tk, hw), 0)
        w_tile = jnp.exp2(slope * (r_sub - (tk - 1)).astype(F32))
        for c in range(n_kv):
            vw = v_ref[c * tk:(c + 1) * tk, :].astype(F32) * w_tile
            vtw_ref[:hw, c * tk:(c + 1) * tk] = vw.T.astype(BF16)
        r_lane = lax.broadcasted_iota(jnp.int32, (ATTN_SUM_ROWS, vtw_ref.shape[1]), 1) % tk
        vtw_ref[hw:, :] = jnp.exp2(slope * (r_lane - (tk - 1)).astype(F32)).astype(BF16)
        kpos = lax.broadcasted_iota(jnp.int32, (tk, 2 * tq), 0)
        qpos = lax.broadcasted_iota(jnp.int32, (tk, 2 * tq), 1) % tq
        for d in range(2):
            mask_ref[d] = jnp.where(kpos + d * tk <= qpos, 0.0, MASK_VALUE)

    q = q_ref[...].astype(F32)
    lane = lax.broadcasted_iota(jnp.int32, q.shape, 1)
    half = hw // 2
    qq = jnp.concatenate([jnp.where(lane < half, q, 0.0), jnp.where(lane >= half, q, 0.0)], axis=0)
    qqt_ref[...] = qq.T.astype(BF16)

    def scores(j):
        k0 = pl.multiple_of(j * tk, tk)
        return _dot(k_ref[pl.ds(k0, tk), :], qqt_ref[...])

    def values_t(j):
        return vtw_ref[:, pl.ds(pl.multiple_of(j * tk, tk), tk)]

    def step(j, cur, diag=None, prefetch=True):
        nxt = 1 - cur
        if prefetch:
            st_refs[nxt][...] = scores(j + 1)
        acc_ref[...] = alpha_ref[...] * acc_ref[...] + _dot(values_t(jnp.maximum(j - 1, 0)), pt_refs[nxt][...])
        st = st_refs[cur][...]
        if diag is not None:
            st = st + mask_ref[diag]
        b_max = slope * (j * tk + (tk - 1)).astype(F32)
        m = m_ref[...]
        m_new = jnp.maximum(m, jnp.max(st, axis=0, keepdims=True) + b_max)
        alpha_ref[...] = jnp.exp2(m - m_new)
        m_ref[...] = m_new
        pt_refs[cur][...] = jnp.exp2(st - (m_new - b_max)).astype(BF16)

    st0_ref[...] = scores(0)
    pt1_ref[...] = jnp.zeros_like(pt1_ref)
    acc_ref[...] = jnp.zeros_like(acc_ref)
    m_ref[...] = jnp.full_like(m_ref, MASK_VALUE)
    alpha_ref[...] = jnp.ones_like(alpha_ref)

    def quad(p, _):
        for t in range(4):
            step(4 * p + t, t % 2)
        return 0

    lax.fori_loop(0, i // 2, quad, 0)

    @pl.when(i % 2 == 1)
    def _():
        step(2 * i - 2, 0)
        step(2 * i - 1, 1)
    step(2 * i, 0, diag=0)
    step(2 * i + 1, 1, diag=1, prefetch=False)
    acc = alpha_ref[...] * acc_ref[...] + _dot(values_t(2 * i + 1), pt1_ref[...])
    ot = acc[:hw] / acc[hw:hw + 1]
    lam = (jnp.exp(jnp.sum(lq1_ref[...] * lk1_ref[...], axis=-1, keepdims=True))
           - jnp.exp(jnp.sum(lq2_ref[...] * lk2_ref[...], axis=-1, keepdims=True)) + lambda_init)
    o = (ot[:, :tq] - lam * ot[:, tq:]).T
    o = _rms(o, sg_ref[...], SUBLN_EPS) * (1.0 - lambda_init)
    o_ref[...] = o.astype(o_ref.dtype)


def _attn(ua, slopes, layer, lq1, lk1, lq2, lk2, subln_g, lambda_init, tq):
    bsz, seq, n_a = ua.shape
    d_attn = n_a // 3
    hw = d_attn // DIFF_HEADS
    tk = tq // 2
    kern = functools.partial(_attn_kernel, tq=tq, tk=tk, lambda_init=lambda_init)
    return pl.pallas_call(
        kern,
        grid=(bsz, DIFF_HEADS, seq // tq),
        in_specs=[pl.BlockSpec(memory_space=pltpu.SMEM),
                  pl.BlockSpec((None, tq, hw), lambda b, h, i: (b, i, h)),
                  pl.BlockSpec((None, seq, hw), lambda b, h, i: (b, 0, DIFF_HEADS + h)),
                  pl.BlockSpec((None, seq, hw), lambda b, h, i: (b, 0, 2 * DIFF_HEADS + h))]
                 + [_layer_spec(a, layer) for a in (lq1, lk1, lq2, lk2, subln_g)],
        out_specs=pl.BlockSpec((None, tq, hw), lambda b, h, i: (b, i, h)),
        out_shape=jax.ShapeDtypeStruct((bsz, seq, d_attn), BF16),
        scratch_shapes=[pltpu.VMEM((hw + ATTN_SUM_ROWS, seq), BF16),
                        pltpu.VMEM((2, tk, 2 * tq), F32),
                        pltpu.VMEM((hw, 2 * tq), BF16),
                        pltpu.VMEM((tk, 2 * tq), F32), pltpu.VMEM((tk, 2 * tq), F32),
                        pltpu.VMEM((tk, 2 * tq), BF16), pltpu.VMEM((tk, 2 * tq), BF16),
                        pltpu.VMEM((hw + ATTN_SUM_ROWS, 2 * tq), F32),
                        pltpu.VMEM((1, 2 * tq), F32), pltpu.VMEM((1, 2 * tq), F32)],
        compiler_params=_params("parallel", "parallel", "arbitrary"),
        name="diff_attn",
    )(slopes, ua, ua, ua, lq1, lk1, lq2, lk2, subln_g)


def _post_kernel(x_ref, oa_ref, ob_ref, ug_ref, p_ref, pa_ref, pb_ref, wo_ref, gm_ref, w1_ref, w2_ref,
                 gp_ref, wp_ref, wg_ref, fg_ref, o_ref, *, ff_blk, final_norm):
    d = x_ref.shape[1]
    merged = (ug_ref[:, :d].astype(F32) * _dot(oa_ref[...], pa_ref[...])
              + ug_ref[:, d:].astype(F32) * _dot(ob_ref[...], pb_ref[...]))
    x = x_ref[...] + _dot(merged.astype(BF16), wo_ref[...])
    h = _rms(x, gm_ref[...], NORM_EPS).astype(BF16)
    acc = x
    for c in range(w1_ref.shape[1] // ff_blk):
        a = jnp.maximum(_dot(h, w1_ref[:, c * ff_blk:(c + 1) * ff_blk]), 0.0)
        acc = acc + _dot((a * a).astype(BF16), w2_ref[c * ff_blk:(c + 1) * ff_blk, :])
    x = acc
    gate = _sigmoid(_dot(_rms(x, gp_ref[...], NORM_EPS).astype(BF16), wg_ref[...]))
    x = x + _dot(p_ref[...].astype(BF16), wp_ref[...]) * gate
    if final_norm:
        x = _rms(x, fg_ref[...], NORM_EPS)
    o_ref[...] = x


def _post(x, oa, ob, ug, p, layer, pa, pb, wo, gm, w1, w2, gp, wp, wg, fg, final_norm, tm, ff_blk):
    n, d = x.shape
    blk = lambda a: pl.BlockSpec((tm, a.shape[1]), lambda i: (i, 0))
    return pl.pallas_call(
        functools.partial(_post_kernel, ff_blk=ff_blk, final_norm=final_norm),
        grid=(n // tm,),
        in_specs=[blk(x), blk(oa), blk(ob), blk(ug),
                  pl.BlockSpec((None, tm, p.shape[2]), lambda i: (layer, i, 0))]
                 + [_layer_spec(a, layer) for a in (pa, pb, wo, gm, w1, w2, gp, wp, wg)]
                 + [_const_spec(fg.shape)],
        out_specs=blk(x),
        out_shape=jax.ShapeDtypeStruct((n, d), F32),
        compiler_params=_params("parallel"),
        name="post",
    )(x, oa, ob, ug, p, pa, pb, wo, gm, w1, w2, gp, wp, wg, fg)


def _pick(n, pref):
    while n % pref:
        pref //= 2
    return pref


def kernel(x, p, norm_mix_g, w_in, rwkv_mu, rwkv_w0, rwkv_w2, rwkv_a0, rwkv_a2, rwkv_g2, rwkv_k_k, rwkv_k_a, rwkv_r_k, rwkv_lnx_w, rwkv_lnx_b, lam_q1, lam_k1, lam_q2, lam_k2, diff_subln_g, w_proj_a, w_proj_b, w_out, norm_mlp_g, w_ff1, w_ff2, norm_ple_g, w_ple, w_ple_gate, final_norm_g):
    bsz, seq, d = x.shape
    depth = w_in.shape[0]
    n = bsz * seq
    width = rwkv_w0.shape[1]
    n_r = rwkv_mu.shape[1]
    n_a = 3 * w_proj_b.shape[1]
    d_lora = rwkv_w2.shape[1]
    assert d_lora == 64 and rwkv_a2.shape[1] == 64 and rwkv_g2.shape[1] == 128

    tm = _pick(seq, 512)
    t_blk = _pick(seq, 4 * RWKV_CHUNK)
    tq = _pick(seq, 512)
    ff_blk = _pick(w_ff1.shape[2], 1024)

    bf = lambda a: a.astype(BF16)
    rows = lambda a: a.reshape(depth, 1, -1).astype(F32)
    wwa = jnp.zeros((depth, 128, 2 * width), F32)
    wwa = bf(wwa.at[:, :64, :width].set(rwkv_w2).at[:, 64:, width:].set(rwkv_a2))
    w_in_b, g2_b = bf(w_in), bf(rwkv_g2)
    pa_b, pb_b, wo_b = bf(w_proj_a), bf(w_proj_b), bf(w_out)
    w1_b, w2_b, wp_b, wg_b = bf(w_ff1), bf(w_ff2), bf(w_ple), bf(w_ple_gate)
    g_mix, g_mlp, g_ple = rows(norm_mix_g), rows(norm_mlp_g), rows(norm_ple_g)
    mu = rows(rwkv_mu)
    rwkv_rows = [rows(a) for a in (rwkv_w0, rwkv_a0, rwkv_k_k, rwkv_k_a, rwkv_r_k, rwkv_lnx_w, rwkv_lnx_b)]
    attn_rows = [rows(a) for a in (lam_q1, lam_k1, lam_q2, lam_k2, diff_subln_g)]
    fg = final_norm_g.reshape(1, -1).astype(F32)
    p3 = p.reshape(depth, n, -1)
    slopes = 2.0 ** (-8.0 * jnp.arange(1, DIFF_HEADS + 1, dtype=F32) / DIFF_HEADS)

    xf = x.reshape(n, d)
    for i in range(depth):
        ur, ua, ug = _in_proj(xf, g_mix, w_in_b, mu, i, seq, n_r, n_a, tm)
        oa = _rwkv(ur.reshape(bsz, seq, n_r), i, *rwkv_rows, wwa, g2_b, t_blk)
        lambda_init = 0.8 - 0.6 * math.exp(-0.3 * i)
        ob = _attn(ua.reshape(bsz, seq, n_a), slopes, i, *attn_rows, lambda_init, tq)
        xf = _post(xf, oa.reshape(n, width), ob.reshape(n, n_a // 3), ug, p3, i,
                   pa_b, pb_b, wo_b, g_mlp, w1_b, w2_b, g_ple, wp_b, wg_b, fg, i == depth - 1, tm, ff_blk)
    return xf.reshape(bsz, seq, d)
```

```python
import functools
import math

import jax
import jax.numpy as jnp
from jax import lax
from jax.experimental import pallas as pl
from jax.experimental.pallas import tpu as pltpu

F32 = jnp.float32
BF16 = jnp.bfloat16

NORM_EPS = 1e-6
SUBLN_EPS = 1e-5
RWKV_HEAD = 64
GN_EPS = RWKV_HEAD * 1e-5
DIFF_HEADS = 4
Q_BLOCK = 128

VMEM_LIMIT_BYTES = 52 * 1024 * 1024
MASK_VALUE = -1e30
LOG2E = math.log2(math.e)
ATTN_SUM_ROWS = 16

_NT = (((1,), (1,)), ((), ()))
_TN = (((0,), (0,)), ((), ()))


def _params(*sem):
    return pltpu.CompilerParams(dimension_semantics=sem, vmem_limit_bytes=VMEM_LIMIT_BYTES)


def _const_spec(shape):
    zeros = (0,) * len(shape)
    return pl.BlockSpec(shape, lambda *_: zeros, pipeline_mode=pl.Buffered(1))


def _layer_spec(stacked, layer):
    idx = (layer,) + (0,) * (stacked.ndim - 1)
    return pl.BlockSpec((None,) + stacked.shape[1:], lambda *_: idx, pipeline_mode=pl.Buffered(1))


def _rms(x, g, eps):
    return x * lax.rsqrt(jnp.mean(x * x, axis=-1, keepdims=True) + eps) * g


def _sigmoid(x):
    return 0.5 * jnp.tanh(0.5 * x) + 0.5


def _dot(a, b):
    return jnp.dot(a, b, preferred_element_type=F32)


def _in_proj_kernel(x_ref, g_ref, w_ref, mu_ref, ur_ref, ua_ref, ug_ref, carry_ref,
                    *, n_r, n_a, d_attn, q_scale, blocks_per_seq):
    i = pl.program_id(0)

    @pl.when(i == 0)
    def _():
        carry_ref[...] = jnp.zeros_like(carry_ref)

    h = _rms(x_ref[...], g_ref[...], NORM_EPS).astype(BF16)
    c = _dot(h, w_ref[:, :n_r])
    row = lax.broadcasted_iota(jnp.int32, (c.shape[0], 1), 0)
    first = jnp.where(i % blocks_per_seq == 0, 0.0, carry_ref[0:1, :])
    prev = jnp.where(row == 0, first, pltpu.roll(c, 1, axis=0))
    carry_ref[0:1, :] = c[c.shape[0] - 1:, :]
    ur_ref[...] = c + (prev - c) * mu_ref[...]
    ua_ref[:, :d_attn] = (_dot(h, w_ref[:, n_r:n_r + d_attn]) * q_scale).astype(BF16)
    ua_ref[:, d_attn:] = _dot(h, w_ref[:, n_r + d_attn:n_r + n_a]).astype(BF16)
    ug_ref[...] = _sigmoid(_dot(h, w_ref[:, n_r + n_a:])).astype(ug_ref.dtype)


def _in_proj(x, g, w, mu, layer, seq, n_r, n_a, tm):
    n, d = x.shape
    n_g = w.shape[2] - n_r - n_a
    d_attn = n_a // 3
    q_scale = float((d_attn // DIFF_HEADS // 2) ** -0.5) * LOG2E
    assert seq % tm == 0
    kern = functools.partial(_in_proj_kernel, n_r=n_r, n_a=n_a, d_attn=d_attn, q_scale=q_scale,
                             blocks_per_seq=seq // tm)
    return pl.pallas_call(
        kern,
        grid=(n // tm,),
        in_specs=[pl.BlockSpec((tm, d), lambda i: (i, 0)),
                  _layer_spec(g, layer),
                  _layer_spec(w, layer),
                  _layer_spec(mu, layer)],
        out_specs=[pl.BlockSpec((tm, n_r), lambda i: (i, 0)),
                   pl.BlockSpec((tm, n_a), lambda i: (i, 0)),
                   pl.BlockSpec((tm, n_g), lambda i: (i, 0))],
        out_shape=[jax.ShapeDtypeStruct((n, n_r), F32),
                   jax.ShapeDtypeStruct((n, n_a), BF16),
                   jax.ShapeDtypeStruct((n, n_g), BF16)],
        scratch_shapes=[pltpu.VMEM((8, n_r), F32)],
        compiler_params=_params("arbitrary"),
        name="in_proj",
    )(x, g, w, mu)


RWKV_CHUNK = 64
DECAY_SCALE = math.exp(-0.5)


def _split_dot(tri, x):
    hi = x.astype(BF16)
    r1 = x - hi.astype(F32)
    mid = r1.astype(BF16)
    lo = (r1 - mid.astype(F32)).astype(BF16)
    return _dot(tri, hi) + _dot(tri, mid) + _dot(tri, lo)


def _rwkv_kernel(u_ref, w0_ref, a0_ref, kk_ref, ka_ref, rk_ref, lnw_ref, lnb_ref,
                 wwa_ref, g2_ref, o_ref, state_ref, *, width, n_heads, n_chunks):
    C = RWKV_CHUNK
    N = RWKV_HEAD

    @pl.when(pl.program_id(1) == 0)
    def _():
        state_ref[...] = jnp.zeros_like(state_ref)

    row = lax.broadcasted_iota(jnp.int32, (C, 1), 0)
    ti = lax.broadcasted_iota(jnp.int32, (C, C), 0)
    si = lax.broadcasted_iota(jnp.int32, (C, C), 1)
    tri = (si <= ti).astype(BF16)
    lane128 = lax.broadcasted_iota(jnp.int32, (C, 128), 1)
    chunks = range(n_chunks)


    pro = []
    for c in chunks:
        x = u_ref[c * C:(c + 1) * C, :]
        r = x[:, :width]
        k = x[:, width:2 * width]
        v = x[:, 2 * width:3 * width]
        wa = x[:, 3 * width:3 * width + 128]
        gd = x[:, 3 * width + 128:]
        wa = jnp.where(lane128 < 64, jnp.tanh(wa), wa)
        lora = _dot(wa.astype(BF16), wwa_ref[...])
        z = w0_ref[...] + lora[:, :width]
        logw = -DECAY_SCALE * _sigmoid(z)
        a_sig = _sigmoid(a0_ref[...] + lora[:, width:])
        k2 = k * (1.0 + (a_sig - 1.0) * ka_ref[...])
        l_inc = _split_dot(tri, logw)
        l_end = l_inc[C - 1:C, :]
        e_inv = jnp.exp(-l_inc)
        d_end = jnp.exp(l_end - l_inc)
        pro.append(dict(
            v=v, a_sig=a_sig, e_inv=e_inv, d_end=d_end,
            g=_dot(_sigmoid(gd).astype(BF16), g2_ref[...]),
            kk=k * kk_ref[...], rk=r * k2 * rk_ref[...],
            e_exc=jnp.exp(l_inc - logw), r_t=r * jnp.exp(l_inc),
            g_end=jnp.exp(l_end),
            k_t=k2 * e_inv, k_h=k2 * d_end))

    lo = lane128 < N

    def bd(x):
        zero = jnp.zeros_like(x)
        return jnp.concatenate([jnp.where(lo, x, zero), jnp.where(lo, zero, x)], axis=0)

    def head_sum(x):
        s_a = jnp.sum(jnp.where(lo, x, 0.0), axis=-1, keepdims=True)
        s_b = jnp.sum(jnp.where(lo, 0.0, x), axis=-1, keepdims=True)
        return jnp.where(lo, s_a, s_b)

    pairs = range(n_heads // 2)
    cols = [slice(128 * p, 128 * (p + 1)) for p in pairs]
    probs = [(c, p) for c in chunks for p in pairs]
    s_idx = lane128 % N
    strict = s_idx < row
    causal = s_idx <= row
    eye2 = (s_idx == row).astype(F32)

    a_bf, r_bf, v_bf, bk_end, nm, aak, gy = ({} for _ in range(7))
    for c, p in probs:
        pr, sl = pro[c], cols[p]
        kk_p = pr["kk"][:, sl]
        kk_n = kk_p * lax.rsqrt(jnp.maximum(head_sum(kk_p * kk_p), 1e-24))
        b_vec = kk_n * pr["a_sig"][:, sl]
        a_bf[c, p] = (-kk_n * pr["e_exc"][:, sl]).astype(BF16)
        r_bf[c, p] = pr["r_t"][:, sl].astype(BF16)
        v_bf[c, p] = pr["v"][:, sl].astype(BF16)
        bk_end[c, p] = jnp.concatenate([(b_vec * pr["d_end"][:, sl]).astype(BF16),
                                        pr["k_h"][:, sl].astype(BF16)], axis=0)
        ar = jnp.concatenate([a_bf[c, p], r_bf[c, p]], axis=0)
        bk = jnp.concatenate([bd((b_vec * pr["e_inv"][:, sl]).astype(BF16)),
                              bd(pr["k_t"][:, sl].astype(BF16))], axis=0)
        gm = lax.dot_general(ar, bk, _NT, preferred_element_type=F32)
        nm[c, p] = jnp.where(strict, gm[:C, :128], 0.0)
        aak[c, p] = jnp.where(strict, gm[:C, 128:], 0.0).astype(BF16)
        gy[c, p] = jnp.concatenate([jnp.where(causal, gm[C:, :128], 0.0).astype(BF16),
                                    jnp.where(causal, gm[C:, 128:], 0.0).astype(BF16)], axis=1)

    inv = {q: eye2 + nm[q] for q in probs}
    p1 = {q: _dot(aak[q], bd(v_bf[q])) for q in probs}
    pw = {q: nm[q].astype(BF16) for q in probs}
    pw = {q: _dot(pw[q], bd(pw[q])).astype(BF16) for q in probs}
    for _ in range(int(math.log2(C)) - 2):
        both = {q: _dot(jnp.concatenate([inv[q].astype(BF16), pw[q]], axis=0), bd(pw[q])) for q in probs}
        inv = {q: inv[q] + both[q][:C] for q in probs}
        pw = {q: both[q][C:].astype(BF16) for q in probs}
    inv_bf = {q: (inv[q] + _dot(inv[q].astype(BF16), bd(pw[q]))).astype(BF16) for q in probs}
    uw = {q: _dot(inv_bf[q], jnp.concatenate([bd(p1[q].astype(BF16)), bd(a_bf[q])], axis=1)) for q in probs}
    u0 = {q: uw[q][:, :128] for q in probs}
    w_m = {q: uw[q][:, 128:].astype(BF16) for q in probs}

    state = [state_ref[p] for p in pairs]
    y = {}
    for c in chunks:
        s_bd = [bd(state[p].astype(BF16)) for p in pairs]
        ws = [lax.dot_general(jnp.concatenate([w_m[c, p], r_bf[c, p]], axis=0), s_bd[p], _NT,
                              preferred_element_type=F32) for p in pairs]
        u_bf = [(ws[p][:C] + u0[c, p]).astype(BF16) for p in pairs]
        for p in pairs:
            y[c, p] = ws[p][C:] + _dot(gy[c, p], jnp.concatenate([bd(u_bf[p]), bd(v_bf[c, p])], axis=0))
        upd = [lax.dot_general(jnp.concatenate([u_bf[p], v_bf[c, p]], axis=0), bk_end[c, p], _TN,
                               preferred_element_type=F32) for p in pairs]
        state = [state[p] * pro[c]["g_end"][:, cols[p]] + jnp.where(lo, upd[p][:C], upd[p][C:]) for p in pairs]
    for p in pairs:
        state_ref[p] = state[p]

    for c in chunks:
        ys = []
        for p in pairs:
            sl = cols[p]
            yc = y[c, p] - head_sum(y[c, p]) * (1.0 / N)
            var = head_sum(yc * yc) * (1.0 / N)
            bonus = head_sum(pro[c]["rk"][:, sl]) * pro[c]["v"][:, sl]
            ys.append(yc * lax.rsqrt(var + GN_EPS) * lnw_ref[:, sl] + lnb_ref[:, sl] + bonus)
        o_ref[c * C:(c + 1) * C, :] = (jnp.concatenate(ys, axis=1) * pro[c]["g"]).astype(o_ref.dtype)


def _rwkv(ur, layer, w0, a0, k_k, k_a, r_k, lnw, lnb, wwa, g2, t_blk):
    bsz, seq, cols = ur.shape
    width = w0.shape[2]
    n_heads = width // RWKV_HEAD
    assert cols == 3 * width + 256 and wwa.shape[1:] == (128, 2 * width) and g2.shape[1:] == (128, width)
    kern = functools.partial(_rwkv_kernel, width=width, n_heads=n_heads, n_chunks=t_blk // RWKV_CHUNK)
    params = (w0, a0, k_k, k_a, r_k, lnw, lnb, wwa, g2)
    return pl.pallas_call(
        kern,
        grid=(bsz, seq // t_blk),
        in_specs=[pl.BlockSpec((None, t_blk, cols), lambda b, i: (b, i, 0))]
                 + [_layer_spec(a, layer) for a in params],
        out_specs=pl.BlockSpec((None, t_blk, width), lambda b, i: (b, i, 0)),
        out_shape=jax.ShapeDtypeStruct((bsz, seq, width), BF16),
        scratch_shapes=[pltpu.VMEM((n_heads // 2, RWKV_HEAD, 2 * RWKV_HEAD), F32)],
        compiler_params=_params("parallel", "arbitrary"),
        name="rwkv7",
    )(ur, *params)


def _attn_kernel(slope_ref, q_ref, k_ref, v_ref, lq1_ref, lk1_ref, lq2_ref, lk2_ref, sg_ref, o_ref,
                 vtw_ref, mask_ref, qqt_ref, st0_ref, st1_ref, pt0_ref, pt1_ref, acc_ref, m_ref, alpha_ref,
                 *, tq, tk, lambda_init):
    i = pl.program_id(2)
    slope = slope_ref[pl.program_id(1)] * LOG2E
    hw = q_ref.shape[1]
    n_kv = v_ref.shape[0] // tk
    st_refs = (st0_ref, st1_ref)
    pt_refs = (pt0_ref, pt1_ref)

    @pl.when(i == 0)
    def _():
        r_sub = lax.broadcasted_iota(jnp.int32, (tk, hw), 0)
        w_tile = jnp.exp2(slope * (r_sub - (tk - 1)).astype(F32))
        for c in range(n_kv):
            vw = v_ref[c * tk:(c + 1) * tk, :].astype(F32) * w_tile
            vtw_ref[:hw, c * tk:(c + 1) * tk] = vw.T.astype(BF16)
        r_lane = lax.broadcasted_iota(jnp.int32, (ATTN_SUM_ROWS, vtw_ref.shape[1]), 1) % tk
        vtw_ref[hw:, :] = jnp.exp2(slope * (r_lane - (tk - 1)).astype(F32)).astype(BF16)
        kpos = lax.broadcasted_iota(jnp.int32, (tk, 2 * tq), 0)
        qpos = lax.broadcasted_iota(jnp.int32, (tk, 2 * tq), 1) % tq
        for d in range(2):
            mask_ref[d] = jnp.where(kpos + d * tk <= qpos, 0.0, MASK_VALUE)

    qt = q_ref[...].astype(F32).T
    first_map = lax.broadcasted_iota(jnp.int32, qt.shape, 0) < hw // 2
    qqt_ref[...] = jnp.concatenate([jnp.where(first_map, qt, 0.0), jnp.where(first_map, 0.0, qt)],
                                   axis=1).astype(BF16)

    def scores(j):
        k0 = pl.multiple_of(j * tk, tk)
        return _dot(k_ref[pl.ds(k0, tk), :], qqt_ref[...])

    def values_t(j):
        return vtw_ref[:, pl.ds(pl.multiple_of(j * tk, tk), tk)]

    late = (slice(tk, tq), slice(tq + tk, 2 * tq))

    def step(j, cur, diag=None, prefetch_late=False):
        nxt = 1 - cur
        if prefetch_late:
            kb = k_ref[pl.ds(pl.multiple_of((j + 1) * tk, tk), tk), :]
            for r in late:
                st_refs[nxt][:, r] = _dot(kb, qqt_ref[:, r])
        else:
            st_refs[nxt][...] = scores(j + 1)
        acc_ref[...] = alpha_ref[...] * acc_ref[...] + _dot(values_t(jnp.maximum(j - 1, 0)), pt_refs[nxt][...])
        st = st_refs[cur][...]
        if diag is not None:
            st = st + mask_ref[diag]
        b_max = slope * (j * tk + (tk - 1)).astype(F32)
        m = m_ref[...]
        m_new = jnp.maximum(m, jnp.max(st, axis=0, keepdims=True) + b_max)
        alpha_ref[...] = jnp.exp2(m - m_new)
        m_ref[...] = m_new
        pt_refs[cur][...] = jnp.exp2(st - (m_new - b_max)).astype(BF16)

    def last_step(j):
        acc_ref[...] = alpha_ref[...] * acc_ref[...] + _dot(values_t(j - 1), pt0_ref[...])
        b_max = slope * (j * tk + (tk - 1)).astype(F32)
        for r in late:
            st = st1_ref[:, r] + mask_ref[1, :, r]
            m = m_ref[:, r]
            m_new = jnp.maximum(m, jnp.max(st, axis=0, keepdims=True) + b_max)
            pt = jnp.exp2(st - (m_new - b_max)).astype(BF16)
            acc_ref[:, r] = jnp.exp2(m - m_new) * acc_ref[:, r] + _dot(values_t(j), pt)

    st0_ref[...] = scores(0)
    pt1_ref[...] = jnp.zeros_like(pt1_ref)
    acc_ref[...] = jnp.zeros_like(acc_ref)
    m_ref[...] = jnp.full_like(m_ref, MASK_VALUE)
    alpha_ref[...] = jnp.ones_like(alpha_ref)

    def quad(p, _):
        for t in range(4):
            step(4 * p + t, t % 2)
        return 0

    lax.fori_loop(0, i // 2, quad, 0)

    @pl.when(i % 2 == 1)
    def _():
        step(2 * i - 2, 0)
        step(2 * i - 1, 1)
    step(2 * i, 0, diag=0, prefetch_late=True)
    last_step(2 * i + 1)
    acc = acc_ref[...]
    ot = acc[:hw] / acc[hw:hw + 1]
    lam = (jnp.exp(jnp.sum(lq1_ref[...] * lk1_ref[...], axis=-1, keepdims=True))
           - jnp.exp(jnp.sum(lq2_ref[...] * lk2_ref[...], axis=-1, keepdims=True)) + lambda_init)
    o = (ot[:, :tq] - lam * ot[:, tq:]).T
    o = _rms(o, sg_ref[...], SUBLN_EPS) * (1.0 - lambda_init)
    o_ref[...] = o.astype(o_ref.dtype)


def _attn(ua, slopes, layer, lq1, lk1, lq2, lk2, subln_g, lambda_init, tq):
    bsz, seq, n_a = ua.shape
    d_attn = n_a // 3
    hw = d_attn // DIFF_HEADS
    tk = tq // 2
    kern = functools.partial(_attn_kernel, tq=tq, tk=tk, lambda_init=lambda_init)
    return pl.pallas_call(
        kern,
        grid=(bsz, DIFF_HEADS, seq // tq),
        in_specs=[pl.BlockSpec(memory_space=pltpu.SMEM),
                  pl.BlockSpec((None, tq, hw), lambda b, h, i: (b, i, h)),
                  pl.BlockSpec((None, seq, hw), lambda b, h, i: (b, 0, DIFF_HEADS + h)),
                  pl.BlockSpec((None, seq, hw), lambda b, h, i: (b, 0, 2 * DIFF_HEADS + h))]
                 + [_layer_spec(a, layer) for a in (lq1, lk1, lq2, lk2, subln_g)],
        out_specs=pl.BlockSpec((None, tq, hw), lambda b, h, i: (b, i, h)),
        out_shape=jax.ShapeDtypeStruct((bsz, seq, d_attn), BF16),
        scratch_shapes=[pltpu.VMEM((hw + ATTN_SUM_ROWS, seq), BF16),
                        pltpu.VMEM((2, tk, 2 * tq), F32),
                        pltpu.VMEM((hw, 2 * tq), BF16),
                        pltpu.VMEM((tk, 2 * tq), F32), pltpu.VMEM((tk, 2 * tq), F32),
                        pltpu.VMEM((tk, 2 * tq), BF16), pltpu.VMEM((tk, 2 * tq), BF16),
                        pltpu.VMEM((hw + ATTN_SUM_ROWS, 2 * tq), F32),
                        pltpu.VMEM((1, 2 * tq), F32), pltpu.VMEM((1, 2 * tq), F32)],
        compiler_params=_params("parallel", "parallel", "arbitrary"),
        name="diff_attn",
    )(slopes, ua, ua, ua, lq1, lk1, lq2, lk2, subln_g)


def _post_kernel(x_ref, oa_ref, ob_ref, ug_ref, p_ref, pa_ref, pb_ref, wo_ref, gm_ref, w1_ref, w2_ref,
                 gp_ref, wp_ref, wg_ref, fg_ref, o_ref, *, ff_blk, final_norm):
    d = x_ref.shape[1]
    merged = (ug_ref[:, :d].astype(F32) * _dot(oa_ref[...], pa_ref[...])
              + ug_ref[:, d:].astype(F32) * _dot(ob_ref[...], pb_ref[...]))
    x = x_ref[...] + _dot(merged.astype(BF16), wo_ref[...])
    h = _rms(x, gm_ref[...], NORM_EPS).astype(BF16)
    acc = x
    for c in range(w1_ref.shape[1] // ff_blk):
        a = jnp.maximum(_dot(h, w1_ref[:, c * ff_blk:(c + 1) * ff_blk]), 0.0)
        acc = acc + _dot((a * a).astype(BF16), w2_ref[c * ff_blk:(c + 1) * ff_blk, :])
    x = acc
    gate = _sigmoid(_dot(_rms(x, gp_ref[...], NORM_EPS).astype(BF16), wg_ref[...]))
    x = x + _dot(p_ref[...].astype(BF16), wp_ref[...]) * gate
    if final_norm:
        x = _rms(x, fg_ref[...], NORM_EPS)
    o_ref[...] = x


def _post(x, oa, ob, ug, p, layer, pa, pb, wo, gm, w1, w2, gp, wp, wg, fg, final_norm, tm, ff_blk):
    n, d = x.shape
    blk = lambda a: pl.BlockSpec((tm, a.shape[1]), lambda i: (i, 0))
    return pl.pallas_call(
        functools.partial(_post_kernel, ff_blk=ff_blk, final_norm=final_norm),
        grid=(n // tm,),
        in_specs=[blk(x), blk(oa), blk(ob), blk(ug),
                  pl.BlockSpec((None, tm, p.shape[2]), lambda i: (layer, i, 0))]
                 + [_layer_spec(a, layer) for a in (pa, pb, wo, gm, w1, w2, gp, wp, wg)]
                 + [_const_spec(fg.shape)],
        out_specs=blk(x),
        out_shape=jax.ShapeDtypeStruct((n, d), F32),
        compiler_params=_params("parallel"),
        name="post",
    )(x, oa, ob, ug, p, pa, pb, wo, gm, w1, w2, gp, wp, wg, fg)


def _pick(n, pref):
    while n % pref:
        pref //= 2
    return pref


def kernel(x, p, norm_mix_g, w_in, rwkv_mu, rwkv_w0, rwkv_w2, rwkv_a0, rwkv_a2, rwkv_g2, rwkv_k_k, rwkv_k_a, rwkv_r_k, rwkv_lnx_w, rwkv_lnx_b, lam_q1, lam_k1, lam_q2, lam_k2, diff_subln_g, w_proj_a, w_proj_b, w_out, norm_mlp_g, w_ff1, w_ff2, norm_ple_g, w_ple, w_ple_gate, final_norm_g):
    bsz, seq, d = x.shape
    depth = w_in.shape[0]
    n = bsz * seq
    width = rwkv_w0.shape[1]
    n_r = rwkv_mu.shape[1]
    n_a = 3 * w_proj_b.shape[1]
    d_lora = rwkv_w2.shape[1]
    assert d_lora == 64 and rwkv_a2.shape[1] == 64 and rwkv_g2.shape[1] == 128

    tm = _pick(seq, 512)
    t_blk = _pick(seq, 4 * RWKV_CHUNK)
    tq = _pick(seq, 512)
    ff_blk = _pick(w_ff1.shape[2], 1024)

    bf = lambda a: a.astype(BF16)
    rows = lambda a: a.reshape(depth, 1, -1).astype(F32)
    wwa = jnp.zeros((depth, 128, 2 * width), F32)
    wwa = bf(wwa.at[:, :64, :width].set(rwkv_w2).at[:, 64:, width:].set(rwkv_a2))
    w_in_b, g2_b = bf(w_in), bf(rwkv_g2)
    pa_b, pb_b, wo_b = bf(w_proj_a), bf(w_proj_b), bf(w_out)
    w1_b, w2_b, wp_b, wg_b = bf(w_ff1), bf(w_ff2), bf(w_ple), bf(w_ple_gate)
    g_mix, g_mlp, g_ple = rows(norm_mix_g), rows(norm_mlp_g), rows(norm_ple_g)
    mu = rows(rwkv_mu)
    rwkv_rows = [rows(a) for a in (rwkv_w0, rwkv_a0, rwkv_k_k, rwkv_k_a, rwkv_r_k, rwkv_lnx_w, rwkv_lnx_b)]
    attn_rows = [rows(a) for a in (lam_q1, lam_k1, lam_q2, lam_k2, diff_subln_g)]
    fg = final_norm_g.reshape(1, -1).astype(F32)
    p3 = p.reshape(depth, n, -1)
    slopes = 2.0 ** (-8.0 * jnp.arange(1, DIFF_HEADS + 1, dtype=F32) / DIFF_HEADS)

    xf = x.reshape(n, d)
    for i in range(depth):
        ur, ua, ug = _in_proj(xf, g_mix, w_in_b, mu, i, seq, n_r, n_a, tm)
        oa = _rwkv(ur.reshape(bsz, seq, n_r), i, *rwkv_rows, wwa, g2_b, t_blk)
        lambda_init = 0.8 - 0.6 * math.exp(-0.3 * i)
        ob = _attn(ua.reshape(bsz, seq, n_a), slopes, i, *attn_rows, lambda_init, tq)
        xf = _post(xf, oa.reshape(n, width), ob.reshape(n, n_a // 3), ug, p3, i,
                   pa_b, pb_b, wo_b, g_mlp, w1_b, w2_b, g_ple, wp_b, wg_b, fg, i == depth - 1, tm, ff_blk)
    return xf.reshape(bsz, seq, d)
```

```python
import functools
import math

import jax
import jax.numpy as jnp
from jax import lax
from jax.experimental import pallas as pl
from jax.experimental.pallas import tpu as pltpu

F32 = jnp.float32
BF16 = jnp.bfloat16

NORM_EPS = 1e-6
SUBLN_EPS = 1e-5
RWKV_HEAD = 64
GN_EPS = RWKV_HEAD * 1e-5
DIFF_HEADS = 4
Q_BLOCK = 128

VMEM_LIMIT_BYTES = 52 * 1024 * 1024
MASK_VALUE = -1e30
LOG2E = math.log2(math.e)
ATTN_SUM_ROWS = 16

_NT = (((1,), (1,)), ((), ()))
_TN = (((0,), (0,)), ((), ()))


def _params(*sem):
    return pltpu.CompilerParams(dimension_semantics=sem, vmem_limit_bytes=VMEM_LIMIT_BYTES)


def _const_spec(shape):
    zeros = (0,) * len(shape)
    return pl.BlockSpec(shape, lambda *_: zeros, pipeline_mode=pl.Buffered(1))


def _layer_spec(stacked, layer):
    idx = (layer,) + (0,) * (stacked.ndim - 1)
    return pl.BlockSpec((None,) + stacked.shape[1:], lambda *_: idx, pipeline_mode=pl.Buffered(1))


def _rms(x, g, eps):
    return x * lax.rsqrt(jnp.mean(x * x, axis=-1, keepdims=True) + eps) * g


def _sigmoid(x):
    return 0.5 * jnp.tanh(0.5 * x) + 0.5


def _dot(a, b):
    return jnp.dot(a, b, preferred_element_type=F32)


def _in_proj_kernel(x_ref, g_ref, w_ref, mu_ref, ur_ref, ua_ref, ug_ref, carry_ref,
                    *, n_r, n_a, d_attn, q_scale, blocks_per_seq):
    i = pl.program_id(0)

    @pl.when(i == 0)
    def _():
        carry_ref[...] = jnp.zeros_like(carry_ref)

    h = _rms(x_ref[...], g_ref[...], NORM_EPS).astype(BF16)
    c = _dot(h, w_ref[:, :n_r])
    row = lax.broadcasted_iota(jnp.int32, (c.shape[0], 1), 0)
    first = jnp.where(i % blocks_per_seq == 0, 0.0, carry_ref[0:1, :])
    prev = jnp.where(row == 0, first, pltpu.roll(c, 1, axis=0))
    carry_ref[0:1, :] = c[c.shape[0] - 1:, :]
    ur_ref[...] = c + (prev - c) * mu_ref[...]
    ua_ref[:, :d_attn] = (_dot(h, w_ref[:, n_r:n_r + d_attn]) * q_scale).astype(BF16)
    ua_ref[:, d_attn:] = _dot(h, w_ref[:, n_r + d_attn:n_r + n_a]).astype(BF16)
    ug_ref[...] = _sigmoid(_dot(h, w_ref[:, n_r + n_a:])).astype(ug_ref.dtype)


def _in_proj(x, g, w, mu, layer, seq, n_r, n_a, tm):
    n, d = x.shape
    n_g = w.shape[2] - n_r - n_a
    d_attn = n_a // 3
    q_scale = float((d_attn // DIFF_HEADS // 2) ** -0.5) * LOG2E
    assert seq % tm == 0
    kern = functools.partial(_in_proj_kernel, n_r=n_r, n_a=n_a, d_attn=d_attn, q_scale=q_scale,
                             blocks_per_seq=seq // tm)
    return pl.pallas_call(
        kern,
        grid=(n // tm,),
        in_specs=[pl.BlockSpec((tm, d), lambda i: (i, 0)),
                  _layer_spec(g, layer),
                  _layer_spec(w, layer),
                  _layer_spec(mu, layer)],
        out_specs=[pl.BlockSpec((tm, n_r), lambda i: (i, 0)),
                   pl.BlockSpec((tm, n_a), lambda i: (i, 0)),
                   pl.BlockSpec((tm, n_g), lambda i: (i, 0))],
        out_shape=[jax.ShapeDtypeStruct((n, n_r), F32),
                   jax.ShapeDtypeStruct((n, n_a), BF16),
                   jax.ShapeDtypeStruct((n, n_g), BF16)],
        scratch_shapes=[pltpu.VMEM((8, n_r), F32)],
        compiler_params=_params("arbitrary"),
        name="in_proj",
    )(x, g, w, mu)


RWKV_CHUNK = 64
DECAY_SCALE = math.exp(-0.5)


def _split_dot(tri, x):
    hi = x.astype(BF16)
    r1 = x - hi.astype(F32)
    mid = r1.astype(BF16)
    lo = (r1 - mid.astype(F32)).astype(BF16)
    return _dot(tri, hi) + _dot(tri, mid) + _dot(tri, lo)


def _rwkv_kernel(u_ref, w0_ref, a0_ref, kk_ref, ka_ref, rk_ref, lnw_ref, lnb_ref,
                 wwa_ref, g2_ref, o_ref, state_ref, *, width, n_heads, n_chunks):
    C = RWKV_CHUNK
    N = RWKV_HEAD

    @pl.when(pl.program_id(1) == 0)
    def _():
        state_ref[...] = jnp.zeros_like(state_ref)

    row = lax.broadcasted_iota(jnp.int32, (C, 1), 0)
    ti = lax.broadcasted_iota(jnp.int32, (C, C), 0)
    si = lax.broadcasted_iota(jnp.int32, (C, C), 1)
    tri = (si <= ti).astype(BF16)
    lane128 = lax.broadcasted_iota(jnp.int32, (C, 128), 1)
    chunks = range(n_chunks)


    pro = []
    for c in chunks:
        x = u_ref[c * C:(c + 1) * C, :]
        r = x[:, :width]
        k = x[:, width:2 * width]
        v = x[:, 2 * width:3 * width]
        wa = x[:, 3 * width:3 * width + 128]
        gd = x[:, 3 * width + 128:]
        wa = jnp.where(lane128 < 64, jnp.tanh(wa), wa)
        lora = _dot(wa.astype(BF16), wwa_ref[...])
        z = w0_ref[...] + lora[:, :width]
        logw = -DECAY_SCALE * _sigmoid(z)
        a_sig = _sigmoid(a0_ref[...] + lora[:, width:])
        k2 = k * (1.0 + (a_sig - 1.0) * ka_ref[...])
        l_inc = _split_dot(tri, logw)
        l_end = l_inc[C - 1:C, :]
        e_inv = jnp.exp(-l_inc)
        d_end = jnp.exp(l_end - l_inc)
        pro.append(dict(
            v=v, a_sig=a_sig, e_inv=e_inv, d_end=d_end,
            g=_dot(_sigmoid(gd).astype(BF16), g2_ref[...]),
            kk=k * kk_ref[...], rk=r * k2 * rk_ref[...],
            e_exc=jnp.exp(l_inc - logw), r_t=r * jnp.exp(l_inc),
            g_end=jnp.exp(l_end),
            k_t=k2 * e_inv, k_h=k2 * d_end))

    lo = lane128 < N

    def bd(x):
        zero = jnp.zeros_like(x)
        return jnp.concatenate([jnp.where(lo, x, zero), jnp.where(lo, zero, x)], axis=0)

    def head_sum(x):
        s_a = jnp.sum(jnp.where(lo, x, 0.0), axis=-1, keepdims=True)
        s_b = jnp.sum(jnp.where(lo, 0.0, x), axis=-1, keepdims=True)
        return jnp.where(lo, s_a, s_b)

    pairs = range(n_heads // 2)
    cols = [slice(128 * p, 128 * (p + 1)) for p in pairs]
    probs = [(c, p) for c in chunks for p in pairs]
    s_idx = lane128 % N
    strict = s_idx < row
    causal = s_idx <= row
    eye2 = (s_idx == row).astype(F32)

    a_bf, r_bf, v_bf, bk_end, nm, aak, gy = ({} for _ in range(7))
    for c, p in probs:
        pr, sl = pro[c], cols[p]
        kk_p = pr["kk"][:, sl]
        kk_n = kk_p * lax.rsqrt(jnp.maximum(head_sum(kk_p * kk_p), 1e-24))
        b_vec = kk_n * pr["a_sig"][:, sl]
        a_bf[c, p] = (-kk_n * pr["e_exc"][:, sl]).astype(BF16)
        r_bf[c, p] = pr["r_t"][:, sl].astype(BF16)
        v_bf[c, p] = pr["v"][:, sl].astype(BF16)
        bk_end[c, p] = jnp.concatenate([(b_vec * pr["d_end"][:, sl]).astype(BF16),
                                        pr["k_h"][:, sl].astype(BF16)], axis=0)
        ar = jnp.concatenate([a_bf[c, p], r_bf[c, p]], axis=0)
        bk = jnp.concatenate([bd((b_vec * pr["e_inv"][:, sl]).astype(BF16)),
                              bd(pr["k_t"][:, sl].astype(BF16))], axis=0)
        gm = lax.dot_general(ar, bk, _NT, preferred_element_type=F32)
        nm[c, p] = jnp.where(strict, gm[:C, :128], 0.0)
        aak[c, p] = jnp.where(strict, gm[:C, 128:], 0.0).astype(BF16)
        gy[c, p] = jnp.concatenate([jnp.where(causal, gm[C:, :128], 0.0).astype(BF16),
                                    jnp.where(causal, gm[C:, 128:], 0.0).astype(BF16)], axis=1)

    inv = {q: eye2 + nm[q] for q in probs}
    p1 = {q: _dot(aak[q], bd(v_bf[q])) for q in probs}
    pw = {q: nm[q].astype(BF16) for q in probs}
    pw = {q: _dot(pw[q], bd(pw[q])).astype(BF16) for q in probs}
    for _ in range(int(math.log2(C)) - 2):
        both = {q: _dot(jnp.concatenate([inv[q].astype(BF16), pw[q]], axis=0), bd(pw[q])) for q in probs}
        inv = {q: inv[q] + both[q][:C] for q in probs}
        pw = {q: both[q][C:].astype(BF16) for q in probs}
    inv_bf = {q: (inv[q] + _dot(inv[q].astype(BF16), bd(pw[q]))).astype(BF16) for q in probs}
    uw = {q: _dot(inv_bf[q], jnp.concatenate([bd(p1[q].astype(BF16)), bd(a_bf[q])], axis=1)) for q in probs}
    u0 = {q: uw[q][:, :128] for q in probs}
    w_m = {q: uw[q][:, 128:].astype(BF16) for q in probs}

    state = [state_ref[p] for p in pairs]
    y = {}
    for c in chunks:
        s_bd = [bd(state[p].astype(BF16)) for p in pairs]
        ws = [lax.dot_general(jnp.concatenate([w_m[c, p], r_bf[c, p]], axis=0), s_bd[p], _NT,
                              preferred_element_type=F32) for p in pairs]
        u_bf = [(ws[p][:C] + u0[c, p]).astype(BF16) for p in pairs]
        for p in pairs:
            y[c, p] = ws[p][C:] + _dot(gy[c, p], jnp.concatenate([bd(u_bf[p]), bd(v_bf[c, p])], axis=0))
        upd = [lax.dot_general(jnp.concatenate([u_bf[p], v_bf[c, p]], axis=0), bk_end[c, p], _TN,
                               preferred_element_type=F32) for p in pairs]
        state = [state[p] * pro[c]["g_end"][:, cols[p]] + jnp.where(lo, upd[p][:C], upd[p][C:]) for p in pairs]
    for p in pairs:
        state_ref[p] = state[p]

    for c in chunks:
        ys = []
        for p in pairs:
            sl = cols[p]
            yc = y[c, p] - head_sum(y[c, p]) * (1.0 / N)
            var = head_sum(yc * yc) * (1.0 / N)
            bonus = head_sum(pro[c]["rk"][:, sl]) * pro[c]["v"][:, sl]
            ys.append(yc * lax.rsqrt(var + GN_EPS) * lnw_ref[:, sl] + lnb_ref[:, sl] + bonus)
        o_ref[c * C:(c + 1) * C, :] = (jnp.concatenate(ys, axis=1) * pro[c]["g"]).astype(o_ref.dtype)


def _rwkv(ur, layer, w0, a0, k_k, k_a, r_k, lnw, lnb, wwa, g2, t_blk):
    bsz, seq, cols = ur.shape
    width = w0.shape[2]
    n_heads = width // RWKV_HEAD
    assert cols == 3 * width + 256 and wwa.shape[1:] == (128, 2 * width) and g2.shape[1:] == (128, width)
    kern = functools.partial(_rwkv_kernel, width=width, n_heads=n_heads, n_chunks=t_blk // RWKV_CHUNK)
    params = (w0, a0, k_k, k_a, r_k, lnw, lnb, wwa, g2)
    return pl.pallas_call(
        kern,
        grid=(bsz, seq // t_blk),
        in_specs=[pl.BlockSpec((None, t_blk, cols), lambda b, i: (b, i, 0))]
                 + [_layer_spec(a, layer) for a in params],
        out_specs=pl.BlockSpec((None, t_blk, width), lambda b, i: (b, i, 0)),
        out_shape=jax.ShapeDtypeStruct((bsz, seq, width), BF16),
        scratch_shapes=[pltpu.VMEM((n_heads // 2, RWKV_HEAD, 2 * RWKV_HEAD), F32)],
        compiler_params=_params("parallel", "arbitrary"),
        name="rwkv7",
    )(ur, *params)


def _attn_kernel(slope_ref, q_ref, k_ref, v_ref, lq1_ref, lk1_ref, lq2_ref, lk2_ref, sg_ref, o_ref,
                 vtw_ref, mask_ref, qqt_ref, st0_ref, st1_ref, pt0_ref, pt1_ref, acc_ref, m_ref, alpha_ref,
                 *, tq, tk, lambda_init):
    i = pl.program_id(2)
    slope = slope_ref[pl.program_id(1)] * LOG2E
    hw = q_ref.shape[1]
    n_kv = v_ref.shape[0] // tk
    st_refs = (st0_ref, st1_ref)
    pt_refs = (pt0_ref, pt1_ref)

    @pl.when(i == 0)
    def _():
        r_sub = lax.broadcasted_iota(jnp.int32, (tk, hw), 0)
        w_tile = jnp.exp2(slope * (r_sub - (tk - 1)).astype(F32))
        for c in range(n_kv):
            vw = v_ref[c * tk:(c + 1) * tk, :].astype(F32) * w_tile
            vtw_ref[:hw, c * tk:(c + 1) * tk] = vw.T.astype(BF16)
        r_lane = lax.broadcasted_iota(jnp.int32, (ATTN_SUM_ROWS, vtw_ref.shape[1]), 1) % tk
        vtw_ref[hw:, :] = jnp.exp2(slope * (r_lane - (tk - 1)).astype(F32)).astype(BF16)
        kpos = lax.broadcasted_iota(jnp.int32, (tk, 2 * tq), 0)
        qpos = lax.broadcasted_iota(jnp.int32, (tk, 2 * tq), 1) % tq
        for d in range(2):
            mask_ref[d] = jnp.where(kpos + d * tk <= qpos, 0.0, MASK_VALUE)
        mask_ref[2] = jnp.zeros((tk, 2 * tq), F32)

    qt = q_ref[...].astype(F32).T
    first_map = lax.broadcasted_iota(jnp.int32, qt.shape, 0) < hw // 2
    qqt_ref[...] = jnp.concatenate([jnp.where(first_map, qt, 0.0), jnp.where(first_map, 0.0, qt)],
                                   axis=1).astype(BF16)

    def scores(j):
        k0 = pl.multiple_of(j * tk, tk)
        return _dot(k_ref[pl.ds(k0, tk), :], qqt_ref[...])

    def values_t(j):
        return vtw_ref[:, pl.ds(pl.multiple_of(j * tk, tk), tk)]

    late = (slice(tk, tq), slice(tq + tk, 2 * tq))

    def step(j, cur, diag=None, prefetch_late=False):
        nxt = 1 - cur
        if prefetch_late:
            kb = k_ref[pl.ds(pl.multiple_of((j + 1) * tk, tk), tk), :]
            for r in late:
                st_refs[nxt][:, r] = _dot(kb, qqt_ref[:, r])
        else:
            st_refs[nxt][...] = scores(j + 1)
        acc_ref[...] = alpha_ref[...] * acc_ref[...] + _dot(values_t(jnp.maximum(j - 1, 0)), pt_refs[nxt][...])
        st = st_refs[cur][...]
        if diag is not None:
            st = st + mask_ref[diag]
        b_max = slope * (j * tk + (tk - 1)).astype(F32)
        m = m_ref[...]
        m_new = jnp.maximum(m, jnp.max(st, axis=0, keepdims=True) + b_max)
        alpha_ref[...] = jnp.exp2(m - m_new)
        m_ref[...] = m_new
        pt_refs[cur][...] = jnp.exp2(st - (m_new - b_max)).astype(BF16)

    def last_step(j):
        acc_ref[...] = alpha_ref[...] * acc_ref[...] + _dot(values_t(j - 1), pt0_ref[...])
        b_max = slope * (j * tk + (tk - 1)).astype(F32)
        for r in late:
            st = st1_ref[:, r] + mask_ref[1, :, r]
            m = m_ref[:, r]
            m_new = jnp.maximum(m, jnp.max(st, axis=0, keepdims=True) + b_max)
            pt = jnp.exp2(st - (m_new - b_max)).astype(BF16)
            acc_ref[:, r] = jnp.exp2(m - m_new) * acc_ref[:, r] + _dot(values_t(j), pt)

    st = scores(0) + mask_ref[jnp.where(i == 0, 0, 2)]
    st1_ref[...] = scores(1)
    m_new = jnp.max(st, axis=0, keepdims=True) + slope * (tk - 1)
    m_ref[...] = m_new
    pt0_ref[...] = jnp.exp2(st - (m_new - slope * (tk - 1))).astype(BF16)
    acc_ref[...] = jnp.zeros_like(acc_ref)
    alpha_ref[...] = jnp.ones_like(alpha_ref)

    @pl.when(i > 0)
    def _():
        def quad(p, _):
            for t in range(4):
                step(4 * p + 1 + t, (t + 1) % 2)
            return 0

        lax.fori_loop(0, (i - 1) // 2, quad, 0)

        @pl.when(i % 2 == 0)
        def _():
            step(2 * i - 3, 1)
            step(2 * i - 2, 0)

        step(2 * i - 1, 1)
        step(2 * i, 0, diag=0, prefetch_late=True)

    last_step(2 * i + 1)
    acc = acc_ref[...]
    ot = acc[:hw] / acc[hw:hw + 1]
    lam = (jnp.exp(jnp.sum(lq1_ref[...] * lk1_ref[...], axis=-1, keepdims=True))
           - jnp.exp(jnp.sum(lq2_ref[...] * lk2_ref[...], axis=-1, keepdims=True)) + lambda_init)
    o = (ot[:, :tq] - lam * ot[:, tq:]).T
    o = _rms(o, sg_ref[...], SUBLN_EPS) * (1.0 - lambda_init)
    o_ref[...] = o.astype(o_ref.dtype)


def _attn(ua, slopes, layer, lq1, lk1, lq2, lk2, subln_g, lambda_init, tq):
    bsz, seq, n_a = ua.shape
    d_attn = n_a // 3
    hw = d_attn // DIFF_HEADS
    tk = tq // 2
    kern = functools.partial(_attn_kernel, tq=tq, tk=tk, lambda_init=lambda_init)
    return pl.pallas_call(
        kern,
        grid=(bsz, DIFF_HEADS, seq // tq),
        in_specs=[pl.BlockSpec(memory_space=pltpu.SMEM),
                  pl.BlockSpec((None, tq, hw), lambda b, h, i: (b, i, h)),
                  pl.BlockSpec((None, seq, hw), lambda b, h, i: (b, 0, DIFF_HEADS + h)),
                  pl.BlockSpec((None, seq, hw), lambda b, h, i: (b, 0, 2 * DIFF_HEADS + h))]
                 + [_layer_spec(a, layer) for a in (lq1, lk1, lq2, lk2, subln_g)],
        out_specs=pl.BlockSpec((None, tq, hw), lambda b, h, i: (b, i, h)),
        out_shape=jax.ShapeDtypeStruct((bsz, seq, d_attn), BF16),
        scratch_shapes=[pltpu.VMEM((hw + ATTN_SUM_ROWS, seq), BF16),
                        pltpu.VMEM((3, tk, 2 * tq), F32),
                        pltpu.VMEM((hw, 2 * tq), BF16),
                        pltpu.VMEM((tk, 2 * tq), F32), pltpu.VMEM((tk, 2 * tq), F32),
                        pltpu.VMEM((tk, 2 * tq), BF16), pltpu.VMEM((tk, 2 * tq), BF16),
                        pltpu.VMEM((hw + ATTN_SUM_ROWS, 2 * tq), F32),
                        pltpu.VMEM((1, 2 * tq), F32), pltpu.VMEM((1, 2 * tq), F32)],
        compiler_params=_params("parallel", "parallel", "arbitrary"),
        name="diff_attn",
    )(slopes, ua, ua, ua, lq1, lk1, lq2, lk2, subln_g)


def _post_kernel(x_ref, oa_ref, ob_ref, ug_ref, p_ref, pa_ref, pb_ref, wo_ref, gm_ref, w1_ref, w2_ref,
                 gp_ref, wp_ref, wg_ref, fg_ref, o_ref, *, ff_blk, final_norm):
    d = x_ref.shape[1]
    merged = (ug_ref[:, :d].astype(F32) * _dot(oa_ref[...], pa_ref[...])
              + ug_ref[:, d:].astype(F32) * _dot(ob_ref[...], pb_ref[...]))
    x = x_ref[...] + _dot(merged.astype(BF16), wo_ref[...])
    h = _rms(x, gm_ref[...], NORM_EPS).astype(BF16)
    acc = x
    for c in range(w1_ref.shape[1] // ff_blk):
        a = jnp.maximum(_dot(h, w1_ref[:, c * ff_blk:(c + 1) * ff_blk]), 0.0)
        acc = acc + _dot((a * a).astype(BF16), w2_ref[c * ff_blk:(c + 1) * ff_blk, :])
    x = acc
    gate = _sigmoid(_dot(_rms(x, gp_ref[...], NORM_EPS).astype(BF16), wg_ref[...]))
    x = x + _dot(p_ref[...].astype(BF16), wp_ref[...]) * gate
    if final_norm:
        x = _rms(x, fg_ref[...], NORM_EPS)
    o_ref[...] = x


def _post(x, oa, ob, ug, p, layer, pa, pb, wo, gm, w1, w2, gp, wp, wg, fg, final_norm, tm, ff_blk):
    n, d = x.shape
    blk = lambda a: pl.BlockSpec((tm, a.shape[1]), lambda i: (i, 0))
    return pl.pallas_call(
        functools.partial(_post_kernel, ff_blk=ff_blk, final_norm=final_norm),
        grid=(n // tm,),
        in_specs=[blk(x), blk(oa), blk(ob), blk(ug),
                  pl.BlockSpec((None, tm, p.shape[2]), lambda i: (layer, i, 0))]
                 + [_layer_spec(a, layer) for a in (pa, pb, wo, gm, w1, w2, gp, wp, wg)]
                 + [_const_spec(fg.shape)],
        out_specs=blk(x),
        out_shape=jax.ShapeDtypeStruct((n, d), F32),
        compiler_params=_params("parallel"),
        name="post",
    )(x, oa, ob, ug, p, pa, pb, wo, gm, w1, w2, gp, wp, wg, fg)


def _pick(n, pref):
    while n % pref:
        pref //= 2
    return pref


def kernel(x, p, norm_mix_g, w_in, rwkv_mu, rwkv_w0, rwkv_w2, rwkv_a0, rwkv_a2, rwkv_g2, rwkv_k_k, rwkv_k_a, rwkv_r_k, rwkv_lnx_w, rwkv_lnx_b, lam_q1, lam_k1, lam_q2, lam_k2, diff_subln_g, w_proj_a, w_proj_b, w_out, norm_mlp_g, w_ff1, w_ff2, norm_ple_g, w_ple, w_ple_gate, final_norm_g):
    bsz, seq, d = x.shape
    depth = w_in.shape[0]
    n = bsz * seq
    width = rwkv_w0.shape[1]
    n_r = rwkv_mu.shape[1]
    n_a = 3 * w_proj_b.shape[1]
    d_lora = rwkv_w2.shape[1]
    assert d_lora == 64 and rwkv_a2.shape[1] == 64 and rwkv_g2.shape[1] == 128

    tm = _pick(seq, 512)
    t_blk = _pick(seq, 4 * RWKV_CHUNK)
    tq = _pick(seq, 512)
    ff_blk = _pick(w_ff1.shape[2], 1024)

    bf = lambda a: a.astype(BF16)
    rows = lambda a: a.reshape(depth, 1, -1).astype(F32)
    wwa = jnp.zeros((depth, 128, 2 * width), F32)
    wwa = bf(wwa.at[:, :64, :width].set(rwkv_w2).at[:, 64:, width:].set(rwkv_a2))
    w_in_b, g2_b = bf(w_in), bf(rwkv_g2)
    pa_b, pb_b, wo_b = bf(w_proj_a), bf(w_proj_b), bf(w_out)
    w1_b, w2_b, wp_b, wg_b = bf(w_ff1), bf(w_ff2), bf(w_ple), bf(w_ple_gate)
    g_mix, g_mlp, g_ple = rows(norm_mix_g), rows(norm_mlp_g), rows(norm_ple_g)
    mu = rows(rwkv_mu)
    rwkv_rows = [rows(a) for a in (rwkv_w0, rwkv_a0, rwkv_k_k, rwkv_k_a, rwkv_r_k, rwkv_lnx_w, rwkv_lnx_b)]
    attn_rows = [rows(a) for a in (lam_q1, lam_k1, lam_q2, lam_k2, diff_subln_g)]
    fg = final_norm_g.reshape(1, -1).astype(F32)
    p3 = p.reshape(depth, n, -1)
    slopes = 2.0 ** (-8.0 * jnp.arange(1, DIFF_HEADS + 1, dtype=F32) / DIFF_HEADS)

    xf = x.reshape(n, d)
    for i in range(depth):
        ur, ua, ug = _in_proj(xf, g_mix, w_in_b, mu, i, seq, n_r, n_a, tm)
        oa = _rwkv(ur.reshape(bsz, seq, n_r), i, *rwkv_rows, wwa, g2_b, t_blk)
        lambda_init = 0.8 - 0.6 * math.exp(-0.3 * i)
        ob = _attn(ua.reshape(bsz, seq, n_a), slopes, i, *attn_rows, lambda_init, tq)
        xf = _post(xf, oa.reshape(n, width), ob.reshape(n, n_a // 3), ug, p3, i,
                   pa_b, pb_b, wo_b, g_mlp, w1_b, w2_b, g_ple, wp_b, wg_b, fg, i == depth - 1, tm, ff_blk)
    return xf.reshape(bsz, seq, d)
```

```python
import functools
import math

import jax
import jax.numpy as jnp
from jax import lax
from jax.experimental import pallas as pl
from jax.experimental.pallas import tpu as pltpu

F32 = jnp.float32
BF16 = jnp.bfloat16

NORM_EPS = 1e-6
SUBLN_EPS = 1e-5
RWKV_HEAD = 64
GN_EPS = RWKV_HEAD * 1e-5
DIFF_HEADS = 4
Q_BLOCK = 128

VMEM_LIMIT_BYTES = 52 * 1024 * 1024
MASK_VALUE = -1e30
LOG2E = math.log2(math.e)
ATTN_SUM_ROWS = 16

_NT = (((1,), (1,)), ((), ()))
_TN = (((0,), (0,)), ((), ()))


def _params(*sem):
    return pltpu.CompilerParams(dimension_semantics=sem, vmem_limit_bytes=VMEM_LIMIT_BYTES)


def _const_spec(shape):
    zeros = (0,) * len(shape)
    return pl.BlockSpec(shape, lambda *_: zeros, pipeline_mode=pl.Buffered(1))


def _layer_spec(stacked, layer):
    idx = (layer,) + (0,) * (stacked.ndim - 1)
    return pl.BlockSpec((None,) + stacked.shape[1:], lambda *_: idx, pipeline_mode=pl.Buffered(1))


def _rms(x, g, eps):
    return x * lax.rsqrt(jnp.mean(x * x, axis=-1, keepdims=True) + eps) * g


def _sigmoid(x):
    return 0.5 * jnp.tanh(0.5 * x) + 0.5


def _dot(a, b):
    return jnp.dot(a, b, preferred_element_type=F32)


def _in_proj_kernel(x_ref, g_ref, w_ref, mu_ref, ur_ref, ua_ref, ug_ref, carry_ref,
                    *, n_r, n_a, d_attn, q_scale, blocks_per_seq):
    i = pl.program_id(0)

    @pl.when(i == 0)
    def _():
        carry_ref[...] = jnp.zeros_like(carry_ref)

    h = _rms(x_ref[...], g_ref[...], NORM_EPS).astype(BF16)
    c = _dot(h, w_ref[:, :n_r].astype(BF16))
    row = lax.broadcasted_iota(jnp.int32, (c.shape[0], 1), 0)
    first = jnp.where(i % blocks_per_seq == 0, 0.0, carry_ref[0:1, :])
    prev = jnp.where(row == 0, first, pltpu.roll(c, 1, axis=0))
    carry_ref[0:1, :] = c[c.shape[0] - 1:, :]
    ur_ref[...] = c + (prev - c) * mu_ref[...]
    ua_ref[:, :d_attn] = (_dot(h, w_ref[:, n_r:n_r + d_attn].astype(BF16)) * q_scale).astype(BF16)
    ua_ref[:, d_attn:] = _dot(h, w_ref[:, n_r + d_attn:n_r + n_a].astype(BF16)).astype(BF16)
    ug_ref[...] = _sigmoid(_dot(h, w_ref[:, n_r + n_a:].astype(BF16))).astype(ug_ref.dtype)


def _in_proj(x, g, w, mu, layer, seq, n_r, n_a, tm):
    n, d = x.shape
    n_g = w.shape[2] - n_r - n_a
    d_attn = n_a // 3
    q_scale = float((d_attn // DIFF_HEADS // 2) ** -0.5) * LOG2E
    assert seq % tm == 0
    kern = functools.partial(_in_proj_kernel, n_r=n_r, n_a=n_a, d_attn=d_attn, q_scale=q_scale,
                             blocks_per_seq=seq // tm)
    return pl.pallas_call(
        kern,
        grid=(n // tm,),
        in_specs=[pl.BlockSpec((tm, d), lambda i: (i, 0)),
                  _layer_spec(g, layer),
                  _layer_spec(w, layer),
                  _layer_spec(mu, layer)],
        out_specs=[pl.BlockSpec((tm, n_r), lambda i: (i, 0)),
                   pl.BlockSpec((tm, n_a), lambda i: (i, 0)),
                   pl.BlockSpec((tm, n_g), lambda i: (i, 0))],
        out_shape=[jax.ShapeDtypeStruct((n, n_r), F32),
                   jax.ShapeDtypeStruct((n, n_a), BF16),
                   jax.ShapeDtypeStruct((n, n_g), BF16)],
        scratch_shapes=[pltpu.VMEM((8, n_r), F32)],
        compiler_params=_params("arbitrary"),
        name="in_proj",
    )(x, g, w, mu)


RWKV_CHUNK = 64
DECAY_SCALE = math.exp(-0.5)


def _split_dot(tri, x):
    hi = x.astype(BF16)
    r1 = x - hi.astype(F32)
    mid = r1.astype(BF16)
    lo = (r1 - mid.astype(F32)).astype(BF16)
    return _dot(tri, hi) + _dot(tri, mid) + _dot(tri, lo)


def _rwkv_kernel(u_ref, w0_ref, a0_ref, kk_ref, ka_ref, rk_ref, lnw_ref, lnb_ref,
                 wwa_ref, g2_ref, o_ref, state_ref, *, width, n_heads, n_chunks):
    C = RWKV_CHUNK
    N = RWKV_HEAD

    @pl.when(pl.program_id(1) == 0)
    def _():
        state_ref[...] = jnp.zeros_like(state_ref)

    row = lax.broadcasted_iota(jnp.int32, (C, 1), 0)
    ti = lax.broadcasted_iota(jnp.int32, (C, C), 0)
    si = lax.broadcasted_iota(jnp.int32, (C, C), 1)
    tri = (si <= ti).astype(BF16)
    lane128 = lax.broadcasted_iota(jnp.int32, (C, 128), 1)
    chunks = range(n_chunks)


    pro = []
    for c in chunks:
        x = u_ref[c * C:(c + 1) * C, :]
        r = x[:, :width]
        k = x[:, width:2 * width]
        v = x[:, 2 * width:3 * width]
        wa = x[:, 3 * width:3 * width + 128]
        gd = x[:, 3 * width + 128:]
        wa = jnp.where(lane128 < 64, jnp.tanh(wa), wa)
        lora = _dot(wa.astype(BF16), wwa_ref[...])
        z = w0_ref[...] + lora[:, :width]
        logw = -DECAY_SCALE * _sigmoid(z)
        a_sig = _sigmoid(a0_ref[...] + lora[:, width:])
        k2 = k * (1.0 + (a_sig - 1.0) * ka_ref[...])
        l_inc = _split_dot(tri, logw)
        l_end = l_inc[C - 1:C, :]
        e_inv = jnp.exp(-l_inc)
        d_end = jnp.exp(l_end - l_inc)
        pro.append(dict(
            v=v, a_sig=a_sig, e_inv=e_inv, d_end=d_end,
            g=_dot(_sigmoid(gd).astype(BF16), g2_ref[...]),
            kk=k * kk_ref[...], rk=r * k2 * rk_ref[...],
            e_exc=jnp.exp(l_inc - logw), r_t=r * jnp.exp(l_inc),
            g_end=jnp.exp(l_end),
            k_t=k2 * e_inv, k_h=k2 * d_end))

    lo = lane128 < N

    def bd(x):
        zero = jnp.zeros_like(x)
        return jnp.concatenate([jnp.where(lo, x, zero), jnp.where(lo, zero, x)], axis=0)

    def head_sum(x):
        s_a = jnp.sum(jnp.where(lo, x, 0.0), axis=-1, keepdims=True)
        s_b = jnp.sum(jnp.where(lo, 0.0, x), axis=-1, keepdims=True)
        return jnp.where(lo, s_a, s_b)

    pairs = range(n_heads // 2)
    cols = [slice(128 * p, 128 * (p + 1)) for p in pairs]
    probs = [(c, p) for c in chunks for p in pairs]
    s_idx = lane128 % N
    strict = s_idx < row
    causal = s_idx <= row
    eye2 = (s_idx == row).astype(F32)

    a_bf, r_bf, v_bf, bk_end, nm, aak, gy = ({} for _ in range(7))
    for c, p in probs:
        pr, sl = pro[c], cols[p]
        kk_p = pr["kk"][:, sl]
        kk_n = kk_p * lax.rsqrt(jnp.maximum(head_sum(kk_p * kk_p), 1e-24))
        b_vec = kk_n * pr["a_sig"][:, sl]
        a_bf[c, p] = (-kk_n * pr["e_exc"][:, sl]).astype(BF16)
        r_bf[c, p] = pr["r_t"][:, sl].astype(BF16)
        v_bf[c, p] = pr["v"][:, sl].astype(BF16)
        bk_end[c, p] = jnp.concatenate([(b_vec * pr["d_end"][:, sl]).astype(BF16),
                                        pr["k_h"][:, sl].astype(BF16)], axis=0)
        ar = jnp.concatenate([a_bf[c, p], r_bf[c, p]], axis=0)
        bk = jnp.concatenate([bd((b_vec * pr["e_inv"][:, sl]).astype(BF16)),
                              bd(pr["k_t"][:, sl].astype(BF16))], axis=0)
        gm = lax.dot_general(ar, bk, _NT, preferred_element_type=F32)
        nm[c, p] = jnp.where(strict, gm[:C, :128], 0.0)
        aak[c, p] = jnp.where(strict, gm[:C, 128:], 0.0).astype(BF16)
        gy[c, p] = jnp.concatenate([jnp.where(causal, gm[C:, :128], 0.0).astype(BF16),
                                    jnp.where(causal, gm[C:, 128:], 0.0).astype(BF16)], axis=1)

    inv = {q: eye2 + nm[q] for q in probs}
    p1 = {q: _dot(aak[q], bd(v_bf[q])) for q in probs}
    pw = {q: nm[q].astype(BF16) for q in probs}
    pw = {q: _dot(pw[q], bd(pw[q])).astype(BF16) for q in probs}
    for _ in range(int(math.log2(C)) - 2):
        both = {q: _dot(jnp.concatenate([inv[q].astype(BF16), pw[q]], axis=0), bd(pw[q])) for q in probs}
        inv = {q: inv[q] + both[q][:C] for q in probs}
        pw = {q: both[q][C:].astype(BF16) for q in probs}
    inv_bf = {q: (inv[q] + _dot(inv[q].astype(BF16), bd(pw[q]))).astype(BF16) for q in probs}
    uw = {q: _dot(inv_bf[q], jnp.concatenate([bd(p1[q].astype(BF16)), bd(a_bf[q])], axis=1)) for q in probs}
    u0 = {q: uw[q][:, :128] for q in probs}
    w_m = {q: uw[q][:, 128:].astype(BF16) for q in probs}

    state = [state_ref[p] for p in pairs]
    y = {}
    for c in chunks:
        s_bd = [bd(state[p].astype(BF16)) for p in pairs]
        ws = [lax.dot_general(jnp.concatenate([w_m[c, p], r_bf[c, p]], axis=0), s_bd[p], _NT,
                              preferred_element_type=F32) for p in pairs]
        u_bf = [(ws[p][:C] + u0[c, p]).astype(BF16) for p in pairs]
        for p in pairs:
            y[c, p] = ws[p][C:] + _dot(gy[c, p], jnp.concatenate([bd(u_bf[p]), bd(v_bf[c, p])], axis=0))
        upd = [lax.dot_general(jnp.concatenate([u_bf[p], v_bf[c, p]], axis=0), bk_end[c, p], _TN,
                               preferred_element_type=F32) for p in pairs]
        state = [state[p] * pro[c]["g_end"][:, cols[p]] + jnp.where(lo, upd[p][:C], upd[p][C:]) for p in pairs]
    for p in pairs:
        state_ref[p] = state[p]

    for c in chunks:
        ys = []
        for p in pairs:
            sl = cols[p]
            yc = y[c, p] - head_sum(y[c, p]) * (1.0 / N)
            var = head_sum(yc * yc) * (1.0 / N)
            bonus = head_sum(pro[c]["rk"][:, sl]) * pro[c]["v"][:, sl]
            ys.append(yc * lax.rsqrt(var + GN_EPS) * lnw_ref[:, sl] + lnb_ref[:, sl] + bonus)
        o_ref[c * C:(c + 1) * C, :] = (jnp.concatenate(ys, axis=1) * pro[c]["g"]).astype(o_ref.dtype)


def _rwkv(ur, layer, w0, a0, k_k, k_a, r_k, lnw, lnb, wwa, g2, t_blk):
    bsz, seq, cols = ur.shape
    width = w0.shape[2]
    n_heads = width // RWKV_HEAD
    assert cols == 3 * width + 256 and wwa.shape[1:] == (128, 2 * width) and g2.shape[1:] == (128, width)
    kern = functools.partial(_rwkv_kernel, width=width, n_heads=n_heads, n_chunks=t_blk // RWKV_CHUNK)
    params = (w0, a0, k_k, k_a, r_k, lnw, lnb, wwa, g2)
    return pl.pallas_call(
        kern,
        grid=(bsz, seq // t_blk),
        in_specs=[pl.BlockSpec((None, t_blk, cols), lambda b, i: (b, i, 0))]
                 + [_layer_spec(a, layer) for a in params],
        out_specs=pl.BlockSpec((None, t_blk, width), lambda b, i: (b, i, 0)),
        out_shape=jax.ShapeDtypeStruct((bsz, seq, width), BF16),
        scratch_shapes=[pltpu.VMEM((n_heads // 2, RWKV_HEAD, 2 * RWKV_HEAD), F32)],
        compiler_params=_params("parallel", "arbitrary"),
        name="rwkv7",
    )(ur, *params)


def _attn_kernel(slope_ref, q_ref, k_ref, v_ref, lq1_ref, lk1_ref, lq2_ref, lk2_ref, sg_ref, o_ref,
                 vtw_ref, mask_ref, qqt_ref, st0_ref, st1_ref, pt0_ref, pt1_ref, acc_ref, m_ref, alpha_ref,
                 *, tq, tk, lambda_init):
    i = pl.program_id(2)
    slope = slope_ref[pl.program_id(1)] * LOG2E
    hw = q_ref.shape[1]
    n_kv = v_ref.shape[0] // tk
    st_refs = (st0_ref, st1_ref)
    pt_refs = (pt0_ref, pt1_ref)

    @pl.when(i == 0)
    def _():
        r_sub = lax.broadcasted_iota(jnp.int32, (tk, hw), 0)
        w_tile = jnp.exp2(slope * (r_sub - (tk - 1)).astype(F32))
        for c in range(n_kv):
            vw = v_ref[c * tk:(c + 1) * tk, :].astype(F32) * w_tile
            vtw_ref[:hw, c * tk:(c + 1) * tk] = vw.T.astype(BF16)
        r_lane = lax.broadcasted_iota(jnp.int32, (ATTN_SUM_ROWS, vtw_ref.shape[1]), 1) % tk
        vtw_ref[hw:, :] = jnp.exp2(slope * (r_lane - (tk - 1)).astype(F32)).astype(BF16)
        kpos = lax.broadcasted_iota(jnp.int32, (tk, 2 * tq), 0)
        qpos = lax.broadcasted_iota(jnp.int32, (tk, 2 * tq), 1) % tq
        for d in range(2):
            mask_ref[d] = jnp.where(kpos + d * tk <= qpos, 0.0, MASK_VALUE)
        mask_ref[2] = jnp.zeros((tk, 2 * tq), F32)

    qt = q_ref[...].astype(F32).T
    first_map = lax.broadcasted_iota(jnp.int32, qt.shape, 0) < hw // 2
    qqt_ref[...] = jnp.concatenate([jnp.where(first_map, qt, 0.0), jnp.where(first_map, 0.0, qt)],
                                   axis=1).astype(BF16)

    def scores(j):
        k0 = pl.multiple_of(j * tk, tk)
        return _dot(k_ref[pl.ds(k0, tk), :], qqt_ref[...])

    def values_t(j):
        return vtw_ref[:, pl.ds(pl.multiple_of(j * tk, tk), tk)]

    late = (slice(tk, tq), slice(tq + tk, 2 * tq))

    def step(j, cur, diag=None, prefetch_late=False):
        nxt = 1 - cur
        if prefetch_late:
            kb = k_ref[pl.ds(pl.multiple_of((j + 1) * tk, tk), tk), :]
            for r in late:
                st_refs[nxt][:, r] = _dot(kb, qqt_ref[:, r])
        else:
            st_refs[nxt][...] = scores(j + 1)
        acc_ref[...] = alpha_ref[...] * acc_ref[...] + _dot(values_t(jnp.maximum(j - 1, 0)), pt_refs[nxt][...])
        st = st_refs[cur][...]
        if diag is not None:
            st = st + mask_ref[diag]
        b_max = slope * (j * tk + (tk - 1)).astype(F32)
        m = m_ref[...]
        m_new = jnp.maximum(m, jnp.max(st, axis=0, keepdims=True) + b_max)
        alpha_ref[...] = jnp.exp2(m - m_new)
        m_ref[...] = m_new
        pt_refs[cur][...] = jnp.exp2(st - (m_new - b_max)).astype(BF16)

    def last_step(j):
        acc_ref[...] = alpha_ref[...] * acc_ref[...] + _dot(values_t(j - 1), pt0_ref[...])
        b_max = slope * (j * tk + (tk - 1)).astype(F32)
        for r in late:
            st = st1_ref[:, r] + mask_ref[1, :, r]
            m = m_ref[:, r]
            m_new = jnp.maximum(m, jnp.max(st, axis=0, keepdims=True) + b_max)
            pt = jnp.exp2(st - (m_new - b_max)).astype(BF16)
            acc_ref[:, r] = jnp.exp2(m - m_new) * acc_ref[:, r] + _dot(values_t(j), pt)

    st = scores(0) + mask_ref[jnp.where(i == 0, 0, 2)]
    st1_ref[...] = scores(1)
    m_new = jnp.max(st, axis=0, keepdims=True) + slope * (tk - 1)
    m_ref[...] = m_new
    pt0_ref[...] = jnp.exp2(st - (m_new - slope * (tk - 1))).astype(BF16)
    acc_ref[...] = jnp.zeros_like(acc_ref)
    alpha_ref[...] = jnp.ones_like(alpha_ref)

    @pl.when(i > 0)
    def _():
        def quad(p, _):
            for t in range(4):
                step(4 * p + 1 + t, (t + 1) % 2)
            return 0

        lax.fori_loop(0, (i - 1) // 2, quad, 0)

        @pl.when(i % 2 == 0)
        def _():
            step(2 * i - 3, 1)
            step(2 * i - 2, 0)

        step(2 * i - 1, 1)
        step(2 * i, 0, diag=0, prefetch_late=True)

    last_step(2 * i + 1)
    acc = acc_ref[...]
    ot = acc[:hw] / acc[hw:hw + 1]
    lam = (jnp.exp(jnp.sum(lq1_ref[...] * lk1_ref[...], axis=-1, keepdims=True))
           - jnp.exp(jnp.sum(lq2_ref[...] * lk2_ref[...], axis=-1, keepdims=True)) + lambda_init)
    o = (ot[:, :tq] - lam * ot[:, tq:]).T
    o = _rms(o, sg_ref[...], SUBLN_EPS) * (1.0 - lambda_init)
    o_ref[...] = o.astype(o_ref.dtype)


def _attn(ua, slopes, layer, lq1, lk1, lq2, lk2, subln_g, lambda_init, tq):
    bsz, seq, n_a = ua.shape
    d_attn = n_a // 3
    hw = d_attn // DIFF_HEADS
    tk = tq // 2
    kern = functools.partial(_attn_kernel, tq=tq, tk=tk, lambda_init=lambda_init)
    return pl.pallas_call(
        kern,
        grid=(bsz, DIFF_HEADS, seq // tq),
        in_specs=[pl.BlockSpec(memory_space=pltpu.SMEM),
                  pl.BlockSpec((None, tq, hw), lambda b, h, i: (b, i, h)),
                  pl.BlockSpec((None, seq, hw), lambda b, h, i: (b, 0, DIFF_HEADS + h)),
                  pl.BlockSpec((None, seq, hw), lambda b, h, i: (b, 0, 2 * DIFF_HEADS + h))]
                 + [_layer_spec(a, layer) for a in (lq1, lk1, lq2, lk2, subln_g)],
        out_specs=pl.BlockSpec((None, tq, hw), lambda b, h, i: (b, i, h)),
        out_shape=jax.ShapeDtypeStruct((bsz, seq, d_attn), BF16),
        scratch_shapes=[pltpu.VMEM((hw + ATTN_SUM_ROWS, seq), BF16),
                        pltpu.VMEM((3, tk, 2 * tq), F32),
                        pltpu.VMEM((hw, 2 * tq), BF16),
                        pltpu.VMEM((tk, 2 * tq), F32), pltpu.VMEM((tk, 2 * tq), F32),
                        pltpu.VMEM((tk, 2 * tq), BF16), pltpu.VMEM((tk, 2 * tq), BF16),
                        pltpu.VMEM((hw + ATTN_SUM_ROWS, 2 * tq), F32),
                        pltpu.VMEM((1, 2 * tq), F32), pltpu.VMEM((1, 2 * tq), F32)],
        compiler_params=_params("parallel", "parallel", "arbitrary"),
        name="diff_attn",
    )(slopes, ua, ua, ua, lq1, lk1, lq2, lk2, subln_g)


def _post_kernel(x_ref, oa_ref, ob_ref, ug_ref, p_ref, pa_ref, pb_ref, wo_ref, gm_ref, w1_ref, w2_ref,
                 gp_ref, wp_ref, wg_ref, fg_ref, o_ref, *, ff_blk, final_norm):
    d = x_ref.shape[1]
    merged = (ug_ref[:, :d].astype(F32) * _dot(oa_ref[...], pa_ref[...])
              + ug_ref[:, d:].astype(F32) * _dot(ob_ref[...], pb_ref[...]))
    x = x_ref[...] + _dot(merged.astype(BF16), wo_ref[...])
    h = _rms(x, gm_ref[...], NORM_EPS).astype(BF16)
    acc = x
    for c in range(w1_ref.shape[1] // ff_blk):
        a = jnp.maximum(_dot(h, w1_ref[:, c * ff_blk:(c + 1) * ff_blk]), 0.0)
        acc = acc + _dot((a * a).astype(BF16), w2_ref[c * ff_blk:(c + 1) * ff_blk, :])
    x = acc
    gate = _sigmoid(_dot(_rms(x, gp_ref[...], NORM_EPS).astype(BF16), wg_ref[...]))
    x = x + _dot(p_ref[...].astype(BF16), wp_ref[...]) * gate
    if final_norm:
        x = _rms(x, fg_ref[...], NORM_EPS)
    o_ref[...] = x


def _post(x, oa, ob, ug, p, layer, pa, pb, wo, gm, w1, w2, gp, wp, wg, fg, final_norm, tm, ff_blk):
    n, d = x.shape
    blk = lambda a: pl.BlockSpec((tm, a.shape[1]), lambda i: (i, 0))
    return pl.pallas_call(
        functools.partial(_post_kernel, ff_blk=ff_blk, final_norm=final_norm),
        grid=(n // tm,),
        in_specs=[blk(x), blk(oa), blk(ob), blk(ug),
                  pl.BlockSpec((None, tm, p.shape[2]), lambda i: (layer, i, 0))]
                 + [_layer_spec(a, layer) for a in (pa, pb, wo, gm, w1, w2, gp, wp, wg)]
                 + [_const_spec(fg.shape)],
        out_specs=blk(x),
        out_shape=jax.ShapeDtypeStruct((n, d), F32),
        compiler_params=_params("parallel"),
        name="post",
    )(x, oa, ob, ug, p, pa, pb, wo, gm, w1, w2, gp, wp, wg, fg)


def _pick(n, pref):
    while n % pref:
        pref //= 2
    return pref


def kernel(x, p, norm_mix_g, w_in, rwkv_mu, rwkv_w0, rwkv_w2, rwkv_a0, rwkv_a2, rwkv_g2, rwkv_k_k, rwkv_k_a, rwkv_r_k, rwkv_lnx_w, rwkv_lnx_b, lam_q1, lam_k1, lam_q2, lam_k2, diff_subln_g, w_proj_a, w_proj_b, w_out, norm_mlp_g, w_ff1, w_ff2, norm_ple_g, w_ple, w_ple_gate, final_norm_g):
    bsz, seq, d = x.shape
    depth = w_in.shape[0]
    n = bsz * seq
    width = rwkv_w0.shape[1]
    n_r = rwkv_mu.shape[1]
    n_a = 3 * w_proj_b.shape[1]
    d_lora = rwkv_w2.shape[1]
    assert d_lora == 64 and rwkv_a2.shape[1] == 64 and rwkv_g2.shape[1] == 128

    tm = _pick(seq, 512)
    t_blk = _pick(seq, 4 * RWKV_CHUNK)
    tq = _pick(seq, 512)
    ff_blk = _pick(w_ff1.shape[2], 1024)

    bf = lambda a: a.astype(BF16)
    rows = lambda a: a.reshape(depth, 1, -1).astype(F32)
    wwa = jnp.zeros((depth, 128, 2 * width), F32)
    wwa = bf(wwa.at[:, :64, :width].set(rwkv_w2).at[:, 64:, width:].set(rwkv_a2))
    w_in_b, g2_b = w_in, bf(rwkv_g2)
    pa_b, pb_b, wo_b = bf(w_proj_a), bf(w_proj_b), bf(w_out)
    w1_b, w2_b, wp_b, wg_b = bf(w_ff1), bf(w_ff2), bf(w_ple), bf(w_ple_gate)
    g_mix, g_mlp, g_ple = rows(norm_mix_g), rows(norm_mlp_g), rows(norm_ple_g)
    mu = rows(rwkv_mu)
    rwkv_rows = [rows(a) for a in (rwkv_w0, rwkv_a0, rwkv_k_k, rwkv_k_a, rwkv_r_k, rwkv_lnx_w, rwkv_lnx_b)]
    attn_rows = [rows(a) for a in (lam_q1, lam_k1, lam_q2, lam_k2, diff_subln_g)]
    fg = final_norm_g.reshape(1, -1).astype(F32)
    p3 = p.reshape(depth, n, -1)
    slopes = 2.0 ** (-8.0 * jnp.arange(1, DIFF_HEADS + 1, dtype=F32) / DIFF_HEADS)

    xf = x.reshape(n, d)
    for i in range(depth):
        ur, ua, ug = _in_proj(xf, g_mix, w_in_b, mu, i, seq, n_r, n_a, tm)
        oa = _rwkv(ur.reshape(bsz, seq, n_r), i, *rwkv_rows, wwa, g2_b, t_blk)
        lambda_init = 0.8 - 0.6 * math.exp(-0.3 * i)
        ob = _attn(ua.reshape(bsz, seq, n_a), slopes, i, *attn_rows, lambda_init, tq)
        xf = _post(xf, oa.reshape(n, width), ob.reshape(n, n_a // 3), ug, p3, i,
                   pa_b, pb_b, wo_b, g_mlp, w1_b, w2_b, g_ple, wp_b, wg_b, fg, i == depth - 1, tm, ff_blk)
    return xf.reshape(bsz, seq, d)
```
